```python
import functools
import jax, jax.numpy as jnp
from jax import lax
import numpy as np

D_MODEL = 1024
BATCH = 32
SEQ = 256
DEPTH = 4
DEC_BATCH = 2
DEC_SEQ = 2048
PAST_LEN = 512

GRID_W = 64
N_MIXERS = 2
N_RG_LAYERS = (DEPTH + 1) // 2
N_POOL_LAYERS = DEPTH // 2
D_RNN = D_MODEL
RNN_HEADS = 8
RNN_HEAD_DIM = D_RNN // RNN_HEADS
CONV_W = 4
RG_C = 8.0
POOL_WINDOWS = (2, 4, 8, 16)
POOL_GROUPS = 4
POOL_GROUP_DIM = D_MODEL // POOL_GROUPS
D_FF = 2816
N_MOD = 9
EPS = 1e-6

kernel_name = 'hybrid_rglru_pool_diffusion_step'


def _rmsnorm(x, g):
    xf = x.astype(jnp.float32)
    y = xf * lax.rsqrt(jnp.mean(xf * xf, axis=-1, keepdims=True) + EPS)
    return (y * g.astype(jnp.float32)).astype(x.dtype)


def _modulate(x, shift, scale):
    return x * (1 + scale) + shift


def _swiglu(x, w1, w3, w2):
    return (jax.nn.silu(x @ w1) * (x @ w3)) @ w2


def _dwconv_centred(x, w, b):
    n = x.shape[1]
    left = (CONV_W - 1) // 2
    right = CONV_W - 1 - left
    xp = jnp.pad(x, ((0, 0), (left, right), (0, 0)))
    out = b
    for k in range(CONV_W):
        out = out + xp[:, k:k + n] * w[k]
    return out


def _lru_combine(earlier, later):
    a1, b1 = earlier
    a2, b2 = later
    return a1 * a2, a2 * b1 + b2


def _rglru_scan(x, w_a, b_a, w_x, b_x, lam, h0):
    bsz, n, _ = x.shape
    xh = x.reshape(bsz, n, RNN_HEADS, RNN_HEAD_DIM)
    r = jax.nn.sigmoid(jnp.einsum('bnhi,hij->bnhj', xh, w_a).reshape(bsz, n, D_RNN) + b_a)
    i = jax.nn.sigmoid(jnp.einsum('bnhi,hij->bnhj', xh, w_x).reshape(bsz, n, D_RNN) + b_x)
    log_a = -RG_C * r.astype(jnp.float32) * jax.nn.softplus(-lam.astype(jnp.float32))
    a = jnp.exp(log_a)
    beta = jnp.sqrt(-jnp.expm1(2.0 * log_a))
    u = beta * (i * x).astype(jnp.float32)
    u = u.at[:, 0].add(a[:, 0] * h0.astype(jnp.float32))
    _, h = lax.associative_scan(_lru_combine, (a, u), axis=1)
    return h.astype(x.dtype), h[:, -1].astype(x.dtype)


def _rg_mixer(h, p, j, h0_fwd, h0_bwd):
    w_in, conv_w, conv_b, w_a, b_a, w_x, b_x, lam, w_out = p
    gate, xr = jnp.split(h @ w_in[j], 2, axis=-1)
    xr = _dwconv_centred(xr, conv_w[j], conv_b[j])
    y_f, s_f = _rglru_scan(xr, w_a[j, 0], b_a[j, 0], w_x[j, 0], b_x[j, 0], lam[j, 0], h0_fwd)
    y_b, s_b = _rglru_scan(xr[:, ::-1], w_a[j, 1], b_a[j, 1], w_x[j, 1], b_x[j, 1], lam[j, 1], h0_bwd)
    y = (y_f + y_b[:, ::-1]) * jax.nn.gelu(gate)
    return y @ w_out[j], jnp.stack([s_f, s_b], axis=1)


def _window_bounds(n, w):
    t = jnp.arange(n)
    lo = jnp.maximum(t - w // 2, 0)
    hi = jnp.minimum(t + (w - w // 2), n)
    return lo, hi


def _pool1d_mean(x, w):
    n = x.shape[1]
    p = jnp.pad(jnp.cumsum(x.astype(jnp.float32), axis=1), ((0, 0), (1, 0), (0, 0)))
    lo, hi = _window_bounds(n, w)
    s = p[:, hi] - p[:, lo]
    return s / (hi - lo).astype(jnp.float32)[None, :, None]


def _pool2d_mean(x, w):
    bsz, n, cg = x.shape
    rows = n // GRID_W
    g = x.astype(jnp.float32).reshape(bsz, rows, GRID_W, cg)
    s = jnp.cumsum(jnp.cumsum(g, axis=1), axis=2)
    s = jnp.pad(s, ((0, 0), (1, 0), (1, 0), (0, 0)))
    rlo, rhi = _window_bounds(rows, w)
    clo, chi = _window_bounds(GRID_W, w)
    sr = s[:, rhi] - s[:, rlo]
    box = sr[:, :, chi] - sr[:, :, clo]
    cnt = ((rhi - rlo)[:, None] * (chi - clo)[None, :]).astype(jnp.float32)[None, :, :, None]
    return (box / cnt).reshape(bsz, n, cg)


def _pool_mixer(h, p, j, on_grid):
    pool_w, pool_scale = p
    bsz, n, _ = h.shape
    hg = h.reshape(bsz, n, POOL_GROUPS, POOL_GROUP_DIM)
    outs = []
    for g, w in enumerate(POOL_WINDOWS):
        xg = hg[:, :, g]
        m = _pool2d_mean(xg, w) if on_grid else _pool1d_mean(xg, w)
        outs.append((m.astype(h.dtype) - xg) @ pool_w[j, g])
    return jnp.concatenate(outs, axis=-1) * pool_scale[j], None


def _layer(x, mod, l, norm_g, ffn_w1, ffn_w3, ffn_w2, mixer):
    sh1, sc1, g1, sh2, sc2, g2, sh3, sc3, g3 = jnp.split(mod[:, None, :].astype(x.dtype), N_MOD, axis=-1)
    h = _modulate(_rmsnorm(x, norm_g[l, 0]), sh1, sc1)
    x = x + 0.5 * g1 * _rmsnorm(_swiglu(h, ffn_w1[l, 0], ffn_w3[l, 0], ffn_w2[l, 0]), norm_g[l, 1])
    h = _modulate(_rmsnorm(x, norm_g[l, 2]), sh2, sc2)
    y, st = mixer(h)
    x = x + g2 * _rmsnorm(y, norm_g[l, 3])
    h = _modulate(_rmsnorm(x, norm_g[l, 4]), sh3, sc3)
    x = x + 0.5 * g3 * _rmsnorm(_swiglu(h, ffn_w1[l, 1], ffn_w3[l, 1], ffn_w2[l, 1]), norm_g[l, 5])
    return x, st


def setup_inputs(seed: int = 0) -> dict:
    key = jax.random.key(seed)
    ks = jax.random.split(key, 24)
    f32 = jnp.float32
    nrm = lambda k, shape, s: jax.random.normal(k, shape, f32) * s
    u = jax.random.uniform(ks[20], (N_RG_LAYERS, 2, D_RNN), f32, minval=0.9, maxval=0.999)
    base = u ** (1.0 / RG_C)
    lam = jnp.log(base) - jnp.log1p(-base)
    return {
        'x_prompt': nrm(ks[0], (BATCH, SEQ, D_MODEL), 1.0),
        'x_sample': nrm(ks[1], (DEC_BATCH, DEC_SEQ, D_MODEL), 1.0),
        'state_rglru': nrm(ks[2], (DEC_BATCH, N_RG_LAYERS, 2, D_RNN), 0.5),
        'c': nrm(ks[3], (DEC_BATCH, D_MODEL), 1.0),
        'c_ctx': nrm(ks[4], (D_MODEL,), 1.0),
        'mod_w': nrm(ks[5], (DEPTH, D_MODEL, N_MOD * D_MODEL), 0.3 * D_MODEL ** -0.5),
        'mod_b': nrm(ks[6], (DEPTH, N_MOD * D_MODEL), 0.02),
        'norm_g': 1.0 + nrm(ks[7], (DEPTH, 6, D_MODEL), 0.05),
        'ffn_w1': nrm(ks[8], (DEPTH, 2, D_MODEL, D_FF), D_MODEL ** -0.5),
        'ffn_w3': nrm(ks[9], (DEPTH, 2, D_MODEL, D_FF), D_MODEL ** -0.5),
        'ffn_w2': nrm(ks[10], (DEPTH, 2, D_FF, D_MODEL), D_FF ** -0.5),
        'rg_w_in': nrm(ks[11], (N_RG_LAYERS, D_MODEL, 2 * D_RNN), D_MODEL ** -0.5),
        'rg_conv_w': nrm(ks[12], (N_RG_LAYERS, CONV_W, D_RNN), CONV_W ** -0.5),
        'rg_conv_b': nrm(ks[13], (N_RG_LAYERS, D_RNN), 0.02),
        'rg_w_a': nrm(ks[14], (N_RG_LAYERS, 2, RNN_HEADS, RNN_HEAD_DIM, RNN_HEAD_DIM), RNN_HEAD_DIM ** -0.5),
        'rg_b_a': nrm(ks[15], (N_RG_LAYERS, 2, D_RNN), 0.02),
        'rg_w_x': nrm(ks[16], (N_RG_LAYERS, 2, RNN_HEADS, RNN_HEAD_DIM, RNN_HEAD_DIM), RNN_HEAD_DIM ** -0.5),
        'rg_b_x': nrm(ks[17], (N_RG_LAYERS, 2, D_RNN), 0.02),
        'rg_lam': lam,
        'rg_w_out': nrm(ks[18], (N_RG_LAYERS, D_RNN, D_MODEL), D_RNN ** -0.5),
        'pool_w': nrm(ks[19], (N_POOL_LAYERS, POOL_GROUPS, POOL_GROUP_DIM, POOL_GROUP_DIM), POOL_GROUP_DIM ** -0.5),
        'pool_scale': 1.0 + nrm(ks[21], (N_POOL_LAYERS, D_MODEL), 0.1),
    }


def reference(x_prompt, x_sample, state_rglru, c, c_ctx, mod_w, mod_b, norm_g, ffn_w1, ffn_w3, ffn_w2,
              rg_w_in, rg_conv_w, rg_conv_b, rg_w_a, rg_b_a, rg_w_x, rg_b_x, rg_lam, rg_w_out,
              pool_w, pool_scale):
    rg_p = (rg_w_in, rg_conv_w, rg_conv_b, rg_w_a, rg_b_a, rg_w_x, rg_b_x, rg_lam, rg_w_out)
    pool_p = (pool_w, pool_scale)
    silu_ctx = jax.nn.silu(c_ctx)[None]
    silu_lat = jax.nn.silu(c)
    zeros = jnp.zeros((x_prompt.shape[0], D_RNN), x_prompt.dtype)
    xc, xs = x_prompt, x_sample
    ctx_states = []
    for l in range(DEPTH):
        mod_c = silu_ctx @ mod_w[l] + mod_b[l]
        mod_s = silu_lat @ mod_w[l] + mod_b[l]
        j = l // N_MIXERS
        if l % N_MIXERS == 0:
            mix_c = functools.partial(_rg_mixer, p=rg_p, j=j, h0_fwd=zeros, h0_bwd=zeros)
            mix_s = functools.partial(_rg_mixer, p=rg_p, j=j,
                                      h0_fwd=state_rglru[:, j, 0], h0_bwd=state_rglru[:, j, 1])
        else:
            mix_c = functools.partial(_pool_mixer, p=pool_p, j=j, on_grid=False)
            mix_s = functools.partial(_pool_mixer, p=pool_p, j=j, on_grid=True)
        xc, st = _layer(xc, mod_c, l, norm_g, ffn_w1, ffn_w3, ffn_w2, mix_c)
        if st is not None:
            ctx_states.append(st)
        xs, _ = _layer(xs, mod_s, l, norm_g, ffn_w1, ffn_w3, ffn_w2, mix_s)
    new_state_rglru = jnp.stack(ctx_states, axis=1)
    return (xc, xs, new_state_rglru)
```

```python
import functools

import jax
import jax.numpy as jnp
from jax import lax
from jax.experimental import pallas as pl
from jax.experimental.pallas import tpu as pltpu

D_MODEL = 1024
DEPTH = 4
D_RNN = 1024
RNN_HEADS = 8
RNN_HEAD_DIM = D_RNN // RNN_HEADS
CONV_W = 4
RG_C = 8.0
GRID_W = 64
POOL_WINDOWS = (2, 4, 8, 16)
POOL_GROUPS = 4
POOL_GROUP_DIM = D_MODEL // POOL_GROUPS
D_FF = 2816
N_MOD = 9
EPS = 1e-6

F32 = jnp.float32
BF16 = jnp.bfloat16

MOD_ROWS = 8
FFN_TM = 1024
FFN_TF = 256
RES_TM = 1024
SEG = 256
N_SEG = 8
MIX_TT = SEG * N_SEG
SEG_PITCH = 264
HP_W = 2 * RNN_HEAD_DIM
N_HP = D_RNN // HP_W
LANES = 128
N_SLAB = HP_W // LANES
VMEM_LIMIT = 56 * 1024 * 1024


def _dot(a, b):
    return jnp.dot(a, b, preferred_element_type=F32)


def _rms(x, g):
    ms = jnp.mean(x * x, axis=-1, keepdims=True)
    return x * lax.rsqrt(ms + EPS) * g


def _sigmoid(x):
    return 0.5 * jnp.tanh(0.5 * x) + 0.5


def _norm_modulate(x, ng_ref, mod_ref, stage):
    g = ng_ref[2 * stage:2 * stage + 1, :]
    shift = mod_ref[3 * stage:3 * stage + 1, :]
    scale = mod_ref[3 * stage + 1:3 * stage + 2, :]
    return _rms(x, g) * (1.0 + scale) + shift


def _mod_row_map(tiles_per_row, row0):
    if tiles_per_row is None:
        return lambda i: row0
    return lambda i: row0 + i // tiles_per_row


def _mod_kernel(c_ref, w_ref, b_ref, o_ref):
    c = c_ref[...]
    s = c * _sigmoid(c)
    o_ref[...] = _dot(s.astype(BF16), w_ref[...].astype(BF16)) + b_ref[...]


def _modulation(cond, mod_w, mod_b):
    tn = D_MODEL
    n_col = (N_MOD * D_MODEL) // tn
    return pl.pallas_call(
        _mod_kernel,
        grid=(DEPTH, n_col),
        in_specs=[
            pl.BlockSpec((MOD_ROWS, D_MODEL), lambda l, n: (0, 0)),
            pl.BlockSpec((None, D_MODEL, tn), lambda l, n: (l, 0, n)),
            pl.BlockSpec((None, 1, tn), lambda l, n: (l, 0, n)),
        ],
        out_specs=pl.BlockSpec((None, MOD_ROWS, tn), lambda l, n: (l, 0, n)),
        out_shape=jax.ShapeDtypeStruct((DEPTH, MOD_ROWS, N_MOD * D_MODEL), F32),
        compiler_params=pltpu.CompilerParams(
            dimension_semantics=("arbitrary", "arbitrary"), vmem_limit_bytes=VMEM_LIMIT),
        name="modulation",
    )(cond, mod_w, mod_b.reshape(DEPTH, 1, N_MOD * D_MODEL))


def _ffn_kernel(x_ref, mod_ref, ng_ref, w1_ref, w3_ref, w2_ref, o_ref, hn_ref, acc_ref, *, stage):
    k = pl.program_id(1)

    @pl.when(k == 0)
    def _():
        hn_ref[...] = _norm_modulate(x_ref[...], ng_ref, mod_ref, stage).astype(BF16)

    hn = hn_ref[...]
    a = _dot(hn, w1_ref[...].astype(BF16))
    b = _dot(hn, w3_ref[...].astype(BF16))
    p = (a * _sigmoid(a) * b).astype(BF16)
    part = _dot(p, w2_ref[...].astype(BF16))

    @pl.when(k == 0)
    def _():
        acc_ref[...] = part

    @pl.when(k > 0)
    def _():
        acc_ref[...] += part

    @pl.when(k == pl.num_programs(1) - 1)
    def _():
        gate = mod_ref[3 * stage + 2:3 * stage + 3, :]
        g_out = ng_ref[2 * stage + 1:2 * stage + 2, :]
        o_ref[...] = x_ref[...] + (0.5 * gate) * _rms(acc_ref[...], g_out)


def _ffn(x, mod, norm_g, w1, w3, w2, *, layer, stage, tiles_per_row, row0):
    t = x.shape[0]
    sub = 0 if stage == 0 else 1
    row = _mod_row_map(tiles_per_row, row0)
    return pl.pallas_call(
        functools.partial(_ffn_kernel, stage=stage),
        grid=(t // FFN_TM, D_FF // FFN_TF),
        in_specs=[
            pl.BlockSpec((FFN_TM, D_MODEL), lambda i, k: (i, 0)),
            pl.BlockSpec((None, None, N_MOD, D_MODEL), lambda i, k: (layer, row(i), 0, 0)),
            pl.BlockSpec((None, 6, D_MODEL), lambda i, k: (layer, 0, 0)),
            pl.BlockSpec((None, None, D_MODEL, FFN_TF), lambda i, k: (layer, sub, 0, k)),
            pl.BlockSpec((None, None, D_MODEL, FFN_TF), lambda i, k: (layer, sub, 0, k)),
            pl.BlockSpec((None, None, FFN_TF, D_MODEL), lambda i, k: (layer, sub, k, 0)),
        ],
        out_specs=pl.BlockSpec((FFN_TM, D_MODEL), lambda i, k: (i, 0)),
        out_shape=jax.ShapeDtypeStruct(x.shape, F32),
        scratch_shapes=[pltpu.VMEM((FFN_TM, D_MODEL), BF16), pltpu.VMEM((FFN_TM, D_MODEL), F32)],
        compiler_params=pltpu.CompilerParams(
            dimension_semantics=("parallel", "arbitrary"), vmem_limit_bytes=VMEM_LIMIT),
        name="ffn",
    )(x, mod, norm_g, w1, w3, w2)


def _resnorm_kernel(x_ref, y_ref, mod_ref, ng_ref, *rest, project):
    if project:
        w_ref, o_ref = rest
        y = _dot(y_ref[...], w_ref[...].astype(BF16))
    else:
        (o_ref,) = rest
        y = y_ref[...]
    gate = mod_ref[5:6, :]
    o_ref[...] = x_ref[...] + gate * _rms(y, ng_ref[3:4, :])


def _resnorm(x, y, mod, norm_g, w_out, *, layer, tiles_per_row, row0, mixer_idx=0):
    t = x.shape[0]
    project = w_out is not None
    row = _mod_row_map(tiles_per_row, row0)
    in_specs = [
        pl.BlockSpec((RES_TM, D_MODEL), lambda i: (i, 0)),
        pl.BlockSpec((RES_TM, D_MODEL), lambda i: (i, 0)),
        pl.BlockSpec((None, None, N_MOD, D_MODEL), lambda i: (layer, row(i), 0, 0)),
        pl.BlockSpec((None, 6, D_MODEL), lambda i: (layer, 0, 0)),
    ]
    args = [x, y, mod, norm_g]
    if project:
        in_specs.append(pl.BlockSpec((None, D_RNN, D_MODEL), lambda i: (mixer_idx, 0, 0)))
        args.append(w_out)
    return pl.pallas_call(
        functools.partial(_resnorm_kernel, project=project),
        grid=(t // RES_TM,),
        in_specs=in_specs,
        out_specs=pl.BlockSpec((RES_TM, D_MODEL), lambda i: (i, 0)),
        out_shape=jax.ShapeDtypeStruct(x.shape, F32),
        compiler_params=pltpu.CompilerParams(
            dimension_semantics=("parallel",), vmem_limit_bytes=VMEM_LIMIT),
        name="resnorm",
    )(*args)


def _seg_rows(k):
    return pl.ds(k * SEG_PITCH, SEG)


def _rg_kernel(x_ref, mod_ref, ng_ref, h0f_ref, h0b_ref, wig_ref, wix_ref, cw_ref, cb_ref, wg_ref,
               ba_ref, bx_ref, lam_ref, *rest, seq_len, chunked):
    if chunked:
        y_ref, hn_ref, gg_ref, a_ref, u_ref = rest
    else:
        y_ref, sf_ref, sb_ref, hn_ref, gg_ref, a_ref, u_ref = rest
    hp = pl.program_id(1)

    @pl.when(hp == 0)
    def _():
        hn_ref[...] = _norm_modulate(x_ref[...], ng_ref, mod_ref, 1).astype(BF16)

    hn = hn_ref[...]
    gg_ref[...] = jax.nn.gelu(_dot(hn, wig_ref[...].astype(BF16)))
    xr = _dot(hn, wix_ref[...].astype(BF16))

    pos = lax.broadcasted_iota(jnp.int32, (MIX_TT, 1), 0) % seq_len
    cw = cw_ref[...]
    xc = cb_ref[...] + cw[1:2, :] * xr
    xc += cw[0:1, :] * jnp.where(pos >= 1, pltpu.roll(xr, 1, axis=0), 0.0)
    xc += cw[2:3, :] * jnp.where(pos < seq_len - 1, pltpu.roll(xr, MIX_TT - 1, axis=0), 0.0)
    xc += cw[3:4, :] * jnp.where(pos < seq_len - 2, pltpu.roll(xr, MIX_TT - 2, axis=0), 0.0)

    z = -lam_ref[...]
    neg_c_softplus = (-RG_C) * (jnp.maximum(z, 0.0) + jnp.log1p(jnp.exp(-jnp.abs(z))))
    for k in range(N_SEG):
        xk = xc[k * SEG:(k + 1) * SEG, :]
        pre = _dot(xk.astype(BF16), wg_ref[...])
        for d in range(2):
            r = _sigmoid(pre[:, (2 * d) * HP_W:(2 * d + 1) * HP_W] + ba_ref[d:d + 1, :])
            i = _sigmoid(pre[:, (2 * d + 1) * HP_W:(2 * d + 2) * HP_W] + bx_ref[d:d + 1, :])
            a = jnp.exp(r * neg_c_softplus[d:d + 1, :])
            u = jnp.sqrt(1.0 - a * a) * (i * xk)
            for s in range(N_SLAB):
                a_ref[d, s, _seg_rows(k), :] = a[:, s * LANES:(s + 1) * LANES]
                u_ref[d, s, _seg_rows(k), :] = u[:, s * LANES:(s + 1) * LANES]

    def step(t, carry):
        hs, ps = carry
        new_h, new_p = [], []
        for d in range(2):
            row = t if d == 0 else SEG - 1 - t
            idx = pl.ds(row, N_SEG, stride=SEG_PITCH)
            for s in range(N_SLAB):
                a = a_ref[d, s, idx, :]
                h = a * hs[d * N_SLAB + s] + u_ref[d, s, idx, :]
                u_ref[d, s, idx, :] = h
                new_h.append(h)
                if chunked:
                    p = a * ps[d * N_SLAB + s]
                    a_ref[d, s, idx, :] = p
                    new_p.append(p)
        return tuple(new_h), tuple(new_p)

    if chunked:
        h_init = tuple(jnp.zeros((N_SEG, LANES), F32) for _ in range(2 * N_SLAB))
        p_init = tuple(jnp.ones((N_SEG, LANES), F32) for _ in range(2 * N_SLAB))
    else:
        h0 = (h0f_ref[...], h0b_ref[...])
        h_init = tuple(h0[d][:, s * LANES:(s + 1) * LANES] for d in range(2) for s in range(N_SLAB))
        p_init = ()
    h_fin, _ = lax.fori_loop(0, SEG, step, (h_init, p_init), unroll=8)

    if chunked:
        b = pl.program_id(0)
        h0 = (h0f_ref[pl.ds(b, 1), :], h0b_ref[pl.ds(b, 1), :])
        for s in range(N_SLAB):
            lanes = slice(s * LANES, (s + 1) * LANES)
            carry = h0[0][:, lanes]
            for k in range(N_SEG):
                h = u_ref[0, s, _seg_rows(k), :] + a_ref[0, s, _seg_rows(k), :] * carry
                u_ref[0, s, _seg_rows(k), :] = h
                carry = h[SEG - 1:SEG, :]
            carry = h0[1][:, lanes]
            for k in reversed(range(N_SEG)):
                hb = u_ref[1, s, _seg_rows(k), :] + a_ref[1, s, _seg_rows(k), :] * carry
                carry = hb[0:1, :]
                rows = slice(k * SEG, (k + 1) * SEG)
                y = (u_ref[0, s, _seg_rows(k), :] + hb) * gg_ref[rows, lanes]
                y_ref[rows, lanes] = y.astype(BF16)
    else:
        for s in range(N_SLAB):
            lanes = slice(s * LANES, (s + 1) * LANES)
            sf_ref[:, lanes] = h_fin[s]
            sb_ref[:, lanes] = h_fin[N_SLAB + s]
            for k in range(N_SEG):
                rows = slice(k * SEG, (k + 1) * SEG)
                y = (u_ref[0, s, _seg_rows(k), :] + u_ref[1, s, _seg_rows(k), :]) * gg_ref[rows, lanes]
                y_ref[rows, lanes] = y.astype(BF16)


def _rg_scan(x, mod, norm_g, h0f, h0b, rg, *, layer, j, seq_len, row0):
    t = x.shape[0]
    chunked = seq_len > SEG
    assert seq_len == (MIX_TT if chunked else SEG)
    w_in, conv_w, conv_b, w_gate, b_a, b_x, lam = rg
    row = _mod_row_map(1 if chunked else None, row0)
    n_state = h0f.shape[0]
    state_block = (n_state, HP_W) if chunked else (N_SEG, HP_W)
    state_map = (lambda i, hp: (0, hp)) if chunked else (lambda i, hp: (i, hp))
    in_specs = [
        pl.BlockSpec((MIX_TT, D_MODEL), lambda i, hp: (i, 0)),
        pl.BlockSpec((None, None, N_MOD, D_MODEL), lambda i, hp: (layer, row(i), 0, 0)),
        pl.BlockSpec((None, 6, D_MODEL), lambda i, hp: (layer, 0, 0)),
        pl.BlockSpec(state_block, state_map),
        pl.BlockSpec(state_block, state_map),
        pl.BlockSpec((None, D_MODEL, HP_W), lambda i, hp: (j, 0, hp)),
        pl.BlockSpec((None, D_MODEL, HP_W), lambda i, hp: (j, 0, N_HP + hp)),
        pl.BlockSpec((None, CONV_W, HP_W), lambda i, hp: (j, 0, hp)),
        pl.BlockSpec((None, 1, HP_W), lambda i, hp: (j, 0, hp)),
        pl.BlockSpec((None, None, HP_W, 4 * HP_W), lambda i, hp: (j, hp, 0, 0)),
        pl.BlockSpec((None, 2, HP_W), lambda i, hp: (j, 0, hp)),
        pl.BlockSpec((None, 2, HP_W), lambda i, hp: (j, 0, hp)),
        pl.BlockSpec((None, 2, HP_W), lambda i, hp: (j, 0, hp)),
    ]
    y_spec = pl.BlockSpec((MIX_TT, HP_W), lambda i, hp: (i, hp))
    y_shape = jax.ShapeDtypeStruct((t, D_RNN), BF16)
    if chunked:
        out_specs, out_shape = y_spec, y_shape
    else:
        s_spec = pl.BlockSpec((N_SEG, HP_W), lambda i, hp: (i, hp))
        s_shape = jax.ShapeDtypeStruct((t // SEG, D_RNN), F32)
        out_specs, out_shape = (y_spec, s_spec, s_spec), (y_shape, s_shape, s_shape)
    scan_shape = (2, N_SLAB, N_SEG * SEG_PITCH, LANES)
    return pl.pallas_call(
        functools.partial(_rg_kernel, seq_len=seq_len, chunked=chunked),
        grid=(t // MIX_TT, N_HP),
        in_specs=in_specs,
        out_specs=out_specs,
        out_shape=out_shape,
        scratch_shapes=[
            pltpu.VMEM((MIX_TT, D_MODEL), BF16),
            pltpu.VMEM((MIX_TT, HP_W), F32),
            pltpu.VMEM(scan_shape, F32),
            pltpu.VMEM(scan_shape, F32),
        ],
        compiler_params=pltpu.CompilerParams(
            dimension_semantics=("parallel", "arbitrary"), vmem_limit_bytes=VMEM_LIMIT),
        name="rg_scan",
    )(x, mod, norm_g, h0f, h0b, w_in, w_in, conv_w, conv_b, w_gate, b_a, b_x, lam)


def _block_diag_pairs(w):
    n = w.shape[0]
    w = w.reshape(n, 2, N_HP, 2, RNN_HEAD_DIM, RNN_HEAD_DIM)
    z = jnp.zeros_like(w[:, :, :, 0])
    top = jnp.concatenate([w[:, :, :, 0], z], axis=-1)
    bot = jnp.concatenate([z, w[:, :, :, 1]], axis=-1)
    return jnp.concatenate([top, bot], axis=-2)


def _axis_window_sum(v, half, unit, pos, length):
    rows = v.shape[0]

    def down(w, s):
        return jnp.where(pos >= s, pltpu.roll(w, s * unit, axis=0), 0.0)

    def up(w, s):
        return jnp.where(pos < length - s, pltpu.roll(w, rows - s * unit, axis=0), 0.0)

    fwd, bwd, s = v, v, 1
    while s < half:
        fwd = fwd + up(fwd, s)
        bwd = bwd + down(bwd, s)
        s *= 2
    return fwd + down(bwd, 1)


def _window_count(half, pos, length):
    return (jnp.minimum(pos + half, length) - jnp.maximum(pos - half, 0)).astype(F32)


def _pool_kernel(x_ref, mod_ref, ng_ref, pw_ref, ps_ref, o_ref, inv_ref, *, on_grid):
    g = pl.program_id(1)

    @pl.when(g == 0)
    def _():
        x = x_ref[...]
        inv_ref[...] = lax.rsqrt(jnp.mean(x * x, axis=-1, keepdims=True) + EPS)

    row_id = lax.broadcasted_iota(jnp.int32, (MIX_TT, 1), 0)
    for gi, w in enumerate(POOL_WINDOWS):
        @pl.when(g == gi)
        def _(gi=gi, w=w):
            lanes = slice(gi * POOL_GROUP_DIM, (gi + 1) * POOL_GROUP_DIM)
            xg = x_ref[:, lanes] * inv_ref[...] * ng_ref[2:3, lanes]
            xg = xg * (1.0 + mod_ref[4:5, lanes]) + mod_ref[3:4, lanes]
            half = w // 2
            if on_grid:
                col = row_id % GRID_W
                n_rows = MIX_TT // GRID_W
                img_row = row_id // GRID_W
                tot = _axis_window_sum(xg, half, 1, col, GRID_W)
                tot = _axis_window_sum(tot, half, GRID_W, img_row, n_rows)
                cnt = _window_count(half, col, GRID_W) * _window_count(half, img_row, n_rows)
            else:
                pos = row_id % SEG
                tot = _axis_window_sum(xg, half, 1, pos, SEG)
                cnt = _window_count(half, pos, SEG)
            diff = tot / cnt - xg
            o_ref[...] = _dot(diff.astype(BF16), pw_ref[...].astype(BF16)) * ps_ref[...]


def _pool_mix(x, mod, norm_g, pool_w, pool_scale, *, layer, j, on_grid, row0):
    t = x.shape[0]
    row = _mod_row_map(1 if on_grid else None, row0)
    return pl.pallas_call(
        functools.partial(_pool_kernel, on_grid=on_grid),
        grid=(t // MIX_TT, POOL_GROUPS),
        in_specs=[
            pl.BlockSpec((MIX_TT, D_MODEL), lambda i, g: (i, 0)),
            pl.BlockSpec((None, None, N_MOD, D_MODEL), lambda i, g: (layer, row(i), 0, 0)),
            pl.BlockSpec((None, 6, D_MODEL), lambda i, g: (layer, 0, 0)),
            pl.BlockSpec((None, None, POOL_GROUP_DIM, POOL_GROUP_DIM), lambda i, g: (j, g, 0, 0)),
            pl.BlockSpec((None, 1, POOL_GROUP_DIM), lambda i, g: (j, 0, g)),
        ],
        out_specs=pl.BlockSpec((MIX_TT, POOL_GROUP_DIM), lambda i, g: (i, g)),
        out_shape=jax.ShapeDtypeStruct((t, D_MODEL), F32),
        scratch_shapes=[pltpu.VMEM((MIX_TT, 1), F32)],
        compiler_params=pltpu.CompilerParams(
            dimension_semantics=("parallel", "arbitrary"), vmem_limit_bytes=VMEM_LIMIT),
        name="pool_mix",
    )(x, mod, norm_g, pool_w, pool_scale)


def kernel(x_prompt, x_sample, state_rglru, c, c_ctx, mod_w, mod_b, norm_g, ffn_w1, ffn_w3, ffn_w2,
           rg_w_in, rg_conv_w, rg_conv_b, rg_w_a, rg_b_a, rg_w_x, rg_b_x, rg_lam, rg_w_out,
           pool_w, pool_scale):
    batch, seq, _ = x_prompt.shape
    dec_batch, dec_seq, _ = x_sample.shape
    assert seq == SEG and dec_seq == MIX_TT and dec_seq % GRID_W == 0
    assert 1 + dec_batch <= MOD_ROWS
    n_rg = rg_w_in.shape[0]

    cond = jnp.zeros((MOD_ROWS, D_MODEL), F32).at[0].set(c_ctx).at[1:1 + dec_batch].set(c)
    mod = _modulation(cond, mod_w, mod_b).reshape(DEPTH, MOD_ROWS, N_MOD, D_MODEL)

    wa, wx = _block_diag_pairs(rg_w_a), _block_diag_pairs(rg_w_x)
    w_gate = jnp.concatenate([wa[:, 0], wx[:, 0], wa[:, 1], wx[:, 1]], axis=-1).astype(BF16)
    rg = (rg_w_in, rg_conv_w, rg_conv_b.reshape(n_rg, 1, D_RNN), w_gate, rg_b_a, rg_b_x, rg_lam)
    pool_scale3 = pool_scale.reshape(pool_scale.shape[0], 1, D_MODEL)
    zeros = jnp.zeros((batch, D_RNN), F32)

    xc = x_prompt.reshape(batch * seq, D_MODEL)
    xs = x_sample.reshape(dec_batch * dec_seq, D_MODEL)
    groups = (
        dict(tiles_per_row=None, row0=0),
        dict(tiles_per_row=dec_seq // FFN_TM, row0=1),
    )
    ctx_states = []
    for l in range(DEPTH):
        j = l // 2
        xc = _ffn(xc, mod, norm_g, ffn_w1, ffn_w3, ffn_w2, layer=l, stage=0, **groups[0])
        xs = _ffn(xs, mod, norm_g, ffn_w1, ffn_w3, ffn_w2, layer=l, stage=0, **groups[1])
        if l % 2 == 0:
            yc, sf, sb = _rg_scan(xc, mod, norm_g, zeros, zeros, rg,
                                  layer=l, j=j, seq_len=seq, row0=0)
            ctx_states.append(jnp.stack([sf, sb], axis=1))
            ys = _rg_scan(xs, mod, norm_g, state_rglru[:, j, 0], state_rglru[:, j, 1], rg,
                          layer=l, j=j, seq_len=dec_seq, row0=1)
            w_out = rg_w_out
        else:
            yc = _pool_mix(xc, mod, norm_g, pool_w, pool_scale3, layer=l, j=j, on_grid=False, row0=0)
            ys = _pool_mix(xs, mod, norm_g, pool_w, pool_scale3, layer=l, j=j, on_grid=True, row0=1)
            w_out = None
        xc = _resnorm(xc, yc, mod, norm_g, w_out, layer=l, mixer_idx=j, **groups[0])
        xs = _resnorm(xs, ys, mod, norm_g, w_out, layer=l, mixer_idx=j, **groups[1])
        xc = _ffn(xc, mod, norm_g, ffn_w1, ffn_w3, ffn_w2, layer=l, stage=2, **groups[0])
        xs = _ffn(xs, mod, norm_g, ffn_w1, ffn_w3, ffn_w2, layer=l, stage=2, **groups[1])
    new_state = jnp.stack(ctx_states, axis=1)
    return (xc.reshape(batch, seq, D_MODEL), xs.reshape(dec_batch, dec_seq, D_MODEL), new_state)
```

```python
import functools

import jax
import jax.numpy as jnp
from jax import lax
from jax.experimental import pallas as pl
from jax.experimental.pallas import tpu as pltpu

D_MODEL = 1024
DEPTH = 4
D_RNN = 1024
RNN_HEADS = 8
RNN_HEAD_DIM = D_RNN // RNN_HEADS
CONV_W = 4
RG_C = 8.0
GRID_W = 64
POOL_WINDOWS = (2, 4, 8, 16)
POOL_GROUPS = 4
POOL_GROUP_DIM = D_MODEL // POOL_GROUPS
D_FF = 2816
N_MOD = 9
EPS = 1e-6

F32 = jnp.float32
BF16 = jnp.bfloat16

MOD_ROWS = 8
FFN_TM = 1024
FFN_TF = 256
FFN_TN = 256
FFN_NA = D_FF // FFN_TF
FFN_NB = D_MODEL // FFN_TN
RES_TM = 1024
SEG = 256
N_SEG = 8
MIX_TT = SEG * N_SEG
SEG_GAP = 4
SEG_PITCH = SEG + SEG_GAP
SCAN_ROWS = SEG_GAP + N_SEG * SEG_PITCH + 4
RG_CHUNK = 2 * SEG
HP_W = 2 * RNN_HEAD_DIM
N_HP = D_RNN // HP_W
LANES = 128
N_SLAB = HP_W // LANES
LOG2_E = 1.4426950408889634
VMEM_LIMIT = 60 * 1024 * 1024


def _dot(a, b):
    return jnp.dot(a, b, preferred_element_type=F32)


def _rms(x, g):
    ms = jnp.mean(x * x, axis=-1, keepdims=True)
    return x * lax.rsqrt(ms + EPS) * g


def _sigmoid(x):
    return 0.5 * jnp.tanh(0.5 * x) + 0.5


def _norm_modulate(x, ng_ref, mod_ref, stage):
    g = ng_ref[2 * stage:2 * stage + 1, :]
    shift = mod_ref[3 * stage:3 * stage + 1, :]
    scale = mod_ref[3 * stage + 1:3 * stage + 2, :]
    inv = lax.rsqrt(jnp.mean(x * x, axis=-1, keepdims=True) + EPS)
    return (x * inv) * (g * (1.0 + scale)) + shift


def _mod_row_map(tiles_per_row, row0):
    if tiles_per_row is None:
        return lambda i: row0
    return lambda i: row0 + i // tiles_per_row


def _mod_kernel(c_ref, w_ref, b_ref, o_ref):
    c = c_ref[...]
    s = c * _sigmoid(c)
    o_ref[...] = _dot(s.astype(BF16), w_ref[...].astype(BF16)) + b_ref[...]


def _modulation(cond, mod_w, mod_b):
    tn = D_MODEL
    n_col = (N_MOD * D_MODEL) // tn
    return pl.pallas_call(
        _mod_kernel,
        grid=(DEPTH, n_col),
        in_specs=[
            pl.BlockSpec((MOD_ROWS, D_MODEL), lambda l, n: (0, 0)),
            pl.BlockSpec((None, D_MODEL, tn), lambda l, n: (l, 0, n)),
            pl.BlockSpec((None, 1, tn), lambda l, n: (l, 0, n)),
        ],
        out_specs=pl.BlockSpec((None, MOD_ROWS, tn), lambda l, n: (l, 0, n)),
        out_shape=jax.ShapeDtypeStruct((DEPTH, MOD_ROWS, N_MOD * D_MODEL), F32),
        compiler_params=pltpu.CompilerParams(
            dimension_semantics=("arbitrary", "arbitrary"), vmem_limit_bytes=VMEM_LIMIT),
        name="modulation",
    )(cond, mod_w, mod_b.reshape(DEPTH, 1, N_MOD * D_MODEL))


def _ffn_kernel(x_ref, mod_ref, ng_ref, w1_ref, w3_ref, w2_ref, o_ref, hn_ref, p_ref, acc_ref, ss_ref,
                *, stage):
    s = pl.program_id(1)

    @pl.when(s == 0)
    def _():
        hn_ref[...] = _norm_modulate(x_ref[...], ng_ref, mod_ref, stage).astype(BF16)

    @pl.when(s < FFN_NA)
    def _():
        hn = hn_ref[...]
        a = _dot(hn, w1_ref[...].astype(BF16))
        b = _dot(hn, w3_ref[...].astype(BF16))
        p_ref[s] = (a * _sigmoid(a) * b).astype(BF16)

    @pl.when(s >= FFN_NA)
    def _():
        n = s - FFN_NA
        acc = _dot(p_ref[0], w2_ref[0:FFN_TF, :].astype(BF16))
        for k in range(1, FFN_NA):
            acc += _dot(p_ref[k], w2_ref[k * FFN_TF:(k + 1) * FFN_TF, :].astype(BF16))
        acc_ref[n] = acc
        sq = jnp.sum(acc * acc, axis=-1, keepdims=True)

        @pl.when(n == 0)
        def _():
            ss_ref[...] = sq

        @pl.when(n > 0)
        def _():
            ss_ref[...] += sq

    @pl.when(s == FFN_NA + FFN_NB - 1)
    def _():
        inv = lax.rsqrt(ss_ref[...] * (1.0 / D_MODEL) + EPS)
        gain = ng_ref[2 * stage + 1:2 * stage + 2, :] * (0.5 * mod_ref[3 * stage + 2:3 * stage + 3, :])
        for n in range(FFN_NB):
            cols = slice(n * FFN_TN, (n + 1) * FFN_TN)
            o_ref[:, cols] = x_ref[:, cols] + (acc_ref[n] * inv) * gain[:, cols]


def _ffn(x, mod, norm_g, w1, w3, w2, *, layer, stage, tiles_per_row, row0):
    t = x.shape[0]
    sub = 0 if stage == 0 else 1
    row = _mod_row_map(tiles_per_row, row0)
    up_map = lambda i, s: (layer, sub, 0, jnp.minimum(s, FFN_NA - 1))
    down_map = lambda i, s: (layer, sub, 0, jnp.maximum(s - FFN_NA, 0))
    return pl.pallas_call(
        functools.partial(_ffn_kernel, stage=stage),
        grid=(t // FFN_TM, FFN_NA + FFN_NB),
        in_specs=[
            pl.BlockSpec((FFN_TM, D_MODEL), lambda i, s: (i, 0)),
            pl.BlockSpec((None, None, N_MOD, D_MODEL), lambda i, s: (layer, row(i), 0, 0)),
            pl.BlockSpec((None, 6, D_MODEL), lambda i, s: (layer, 0, 0)),
            pl.BlockSpec((None, None, D_MODEL, FFN_TF), up_map),
            pl.BlockSpec((None, None, D_MODEL, FFN_TF), up_map),
            pl.BlockSpec((None, None, D_FF, FFN_TN), down_map),
        ],
        out_specs=pl.BlockSpec((FFN_TM, D_MODEL), lambda i, s: (i, 0)),
        out_shape=jax.ShapeDtypeStruct(x.shape, F32),
        scratch_shapes=[
            pltpu.VMEM((FFN_TM, D_MODEL), BF16),
            pltpu.VMEM((FFN_NA, FFN_TM, FFN_TF), BF16),
            pltpu.VMEM((FFN_NB, FFN_TM, FFN_TN), F32),
            pltpu.VMEM((FFN_TM, 1), F32),
        ],
        compiler_params=pltpu.CompilerParams(
            dimension_semantics=("parallel", "arbitrary"), vmem_limit_bytes=VMEM_LIMIT),
        name="ffn",
    )(x, mod, norm_g, w1, w3, w2)


def _resnorm_kernel(x_ref, y_ref, mod_ref, ng_ref, *rest, project):
    if project:
        w_ref, o_ref = rest
        y = _dot(y_ref[...], w_ref[...].astype(BF16))
    else:
        (o_ref,) = rest
        y = y_ref[...]
    gate = mod_ref[5:6, :]
    o_ref[...] = x_ref[...] + gate * _rms(y, ng_ref[3:4, :])


def _resnorm(x, y, mod, norm_g, w_out, *, layer, tiles_per_row, row0, mixer_idx=0):
    t = x.shape[0]
    project = w_out is not None
    row = _mod_row_map(tiles_per_row, row0)
    in_specs = [
        pl.BlockSpec((RES_TM, D_MODEL), lambda i: (i, 0)),
        pl.BlockSpec((RES_TM, D_MODEL), lambda i: (i, 0)),
        pl.BlockSpec((None, None, N_MOD, D_MODEL), lambda i: (layer, row(i), 0, 0)),
        pl.BlockSpec((None, 6, D_MODEL), lambda i: (layer, 0, 0)),
    ]
    args = [x, y, mod, norm_g]
    if project:
        in_specs.append(pl.BlockSpec((None, D_RNN, D_MODEL), lambda i: (mixer_idx, 0, 0)))
        args.append(w_out)
    return pl.pallas_call(
        functools.partial(_resnorm_kernel, project=project),
        grid=(t // RES_TM,),
        in_specs=in_specs,
        out_specs=pl.BlockSpec((RES_TM, D_MODEL), lambda i: (i, 0)),
        out_shape=jax.ShapeDtypeStruct(x.shape, F32),
        compiler_params=pltpu.CompilerParams(
            dimension_semantics=("parallel",), vmem_limit_bytes=VMEM_LIMIT),
        name="resnorm",
    )(*args)


def _seg_base(k):
    return SEG_GAP + k * SEG_PITCH


def _seg_rows(k, offset=0):
    return pl.ds(_seg_base(k) + offset, SEG)


def _rg_kernel(x_ref, mod_ref, ng_ref, h0f_ref, h0b_ref, wig_ref, wix_ref, cw_ref, cb_ref, wg_ref,
               ba_ref, bx_ref, lam_ref, *rest, chunked):
    if chunked:
        y_ref, hn_ref, gg_ref, xr_ref, a_ref, u_ref, h_ref, p_ref = rest
    else:
        y_ref, sf_ref, sb_ref, hn_ref, gg_ref, xr_ref, a_ref, u_ref, h_ref = rest
    hp = pl.program_id(1)

    @pl.when(hp == 0)
    def _():
        hn_ref[...] = _norm_modulate(x_ref[...], ng_ref, mod_ref, 1).astype(BF16)

    wig = wig_ref[...].astype(BF16)
    wix = wix_ref[...].astype(BF16)
    seg_per_chunk = RG_CHUNK // SEG
    zero2 = jnp.zeros((2, LANES), F32)

    def project(c):
        rows = slice(c * RG_CHUNK, (c + 1) * RG_CHUNK)
        hn = hn_ref[rows, :]
        gg_ref[rows, :] = jax.nn.gelu(_dot(hn, wig))
        xr = _dot(hn, wix)
        for s in range(N_SLAB):
            lanes = slice(s * LANES, (s + 1) * LANES)
            for kk in range(seg_per_chunk):
                k = c * seg_per_chunk + kk
                base = _seg_base(k)
                seg = xr[kk * SEG:(kk + 1) * SEG, lanes]
                xr_ref[s, _seg_rows(k), :] = seg
                inside = chunked and k > 0
                xr_ref[s, base - SEG_GAP:base - 2, :] = seg[0:2, :] if inside else zero2
                inside = chunked and k < N_SEG - 1
                xr_ref[s, base + SEG + 2:base + SEG + SEG_GAP, :] = seg[SEG - 2:SEG, :] if inside else zero2
                if k == 0:
                    xr_ref[s, base - 2:base, :] = zero2
                if k == N_SEG - 1:
                    xr_ref[s, base + SEG:base + SEG + 2, :] = zero2

    z = -lam_ref[...]
    softplus = jnp.maximum(z, 0.0) + jnp.log1p(jnp.exp(-jnp.abs(z)))
    half_rate = (0.5 * (-RG_C)) * softplus
    half_ba = 0.5 * ba_ref[...]
    half_bx = 0.5 * bx_ref[...]
    cw = cw_ref[...]
    cb = cb_ref[...]

    def gates(k):
        taps = []
        for s in range(N_SLAB):
            lanes = slice(s * LANES, (s + 1) * LANES)
            acc = cb[:, lanes] + cw[0:1, lanes] * xr_ref[s, _seg_rows(k, -1), :]
            for j in range(1, CONV_W):
                acc += cw[j:j + 1, lanes] * xr_ref[s, _seg_rows(k, j - 1), :]
            taps.append(acc)
        xk = jnp.concatenate(taps, axis=1)
        pre = _dot(xk.astype(BF16), wg_ref[...])
        xh = 0.5 * xk
        for d in range(2):
            tr = jnp.tanh(pre[:, (2 * d) * HP_W:(2 * d + 1) * HP_W] + half_ba[d:d + 1, :])
            ti = jnp.tanh(pre[:, (2 * d + 1) * HP_W:(2 * d + 2) * HP_W] + half_bx[d:d + 1, :])
            a = jnp.exp(tr * half_rate[d:d + 1, :] + half_rate[d:d + 1, :])
            v = 1.0 - a * a
            beta = jnp.where(v > 0.0, v * lax.rsqrt(v), 0.0)
            u = beta * (xh * ti + xh)
            for s in range(N_SLAB):
                lanes = slice(s * LANES, (s + 1) * LANES)
                a_ref[d, s, _seg_rows(k), :] = a[:, lanes]
                u_ref[d, s, _seg_rows(k), :] = u[:, lanes]

    n_chunk = MIX_TT // RG_CHUNK
    project(0)
    for c in range(n_chunk):
        if c + 1 < n_chunk:
            project(c + 1)
        first = c * seg_per_chunk - 1
        for k in range(max(first, 0), first + seg_per_chunk):
            gates(k)
    gates(N_SEG - 1)

    def step(t, carry):
        hs, ps = carry
        new_h, new_p = [], []
        for d in range(2):
            row = SEG_GAP + (t if d == 0 else SEG - 1 - t)
            idx = pl.ds(row, N_SEG, stride=SEG_PITCH)
            for s in range(N_SLAB):
                a = a_ref[d, s, idx, :]
                h = a * hs[d * N_SLAB + s] + u_ref[d, s, idx, :]
                h_ref[d, s, idx, :] = h
                new_h.append(h)
                if chunked:
                    p = a * ps[d * N_SLAB + s]
                    p_ref[d, s, idx, :] = p
                    new_p.append(p)
        return tuple(new_h), tuple(new_p)

    if chunked:
        h_init = tuple(jnp.zeros((N_SEG, LANES), F32) for _ in range(2 * N_SLAB))
        p_init = tuple(jnp.ones((N_SEG, LANES), F32) for _ in range(2 * N_SLAB))
    else:
        h0 = (h0f_ref[...], h0b_ref[...])
        h_init = tuple(h0[d][:, s * LANES:(s + 1) * LANES] for d in range(2) for s in range(N_SLAB))
        p_init = ()
    h_fin, _ = lax.fori_loop(0, SEG, step, (h_init, p_init), unroll=8)

    if chunked:
        b = pl.program_id(0)
        h0 = (h0f_ref[pl.ds(b, 1), :], h0b_ref[pl.ds(b, 1), :])
        for s in range(N_SLAB):
            lanes = slice(s * LANES, (s + 1) * LANES)
            carry = h0[0][:, lanes]
            for k in range(N_SEG):
                h = h_ref[0, s, _seg_rows(k), :] + p_ref[0, s, _seg_rows(k), :] * carry
                h_ref[0, s, _seg_rows(k), :] = h
                carry = h[SEG - 1:SEG, :]
            carry = h0[1][:, lanes]
            for k in reversed(range(N_SEG)):
                hb = h_ref[1, s, _seg_rows(k), :] + p_ref[1, s, _seg_rows(k), :] * carry
                carry = hb[0:1, :]
                rows = slice(k * SEG, (k + 1) * SEG)
                y = (h_ref[0, s, _seg_rows(k), :] + hb) * gg_ref[rows, lanes]
                y_ref[rows, lanes] = y.astype(BF16)
    else:
        for s in range(N_SLAB):
            lanes = slice(s * LANES, (s + 1) * LANES)
            sf_ref[:, lanes] = h_fin[s]
            sb_ref[:, lanes] = h_fin[N_SLAB + s]
            for k in range(N_SEG):
                rows = slice(k * SEG, (k + 1) * SEG)
                y = (h_ref[0, s, _seg_rows(k), :] + h_ref[1, s, _seg_rows(k), :]) * gg_ref[rows, lanes]
                y_ref[rows, lanes] = y.astype(BF16)


def _rg_scan(x, mod, norm_g, h0f, h0b, rg, *, layer, j, seq_len, row0):
    t = x.shape[0]
    chunked = seq_len > SEG
    assert seq_len == (MIX_TT if chunked else SEG)
    w_in, conv_w, conv_b, w_gate, b_a, b_x, lam = rg
    row = _mod_row_map(1 if chunked else None, row0)
    n_state = h0f.shape[0]
    state_block = (n_state, HP_W) if chunked else (N_SEG, HP_W)
    state_map = (lambda i, hp: (0, hp)) if chunked else (lambda i, hp: (i, hp))
    in_specs = [
        pl.BlockSpec((MIX_TT, D_MODEL), lambda i, hp: (i, 0)),
        pl.BlockSpec((None, None, N_MOD, D_MODEL), lambda i, hp: (layer, row(i), 0, 0)),
        pl.BlockSpec((None, 6, D_MODEL), lambda i, hp: (layer, 0, 0)),
        pl.BlockSpec(state_block, state_map),
        pl.BlockSpec(state_block, state_map),
        pl.BlockSpec((None, D_MODEL, HP_W), lambda i, hp: (j, 0, hp)),
        pl.BlockSpec((None, D_MODEL, HP_W), lambda i, hp: (j, 0, N_HP + hp)),
        pl.BlockSpec((None, CONV_W, HP_W), lambda i, hp: (j, 0, hp)),
        pl.BlockSpec((None, 1, HP_W), lambda i, hp: (j, 0, hp)),
        pl.BlockSpec((None, None, HP_W, 4 * HP_W), lambda i, hp: (j, hp, 0, 0)),
        pl.BlockSpec((None, 2, HP_W), lambda i, hp: (j, 0, hp)),
        pl.BlockSpec((None, 2, HP_W), lambda i, hp: (j, 0, hp)),
        pl.BlockSpec((None, 2, HP_W), lambda i, hp: (j, 0, hp)),
    ]
    y_spec = pl.BlockSpec((MIX_TT, HP_W), lambda i, hp: (i, hp))
    y_shape = jax.ShapeDtypeStruct((t, D_RNN), BF16)
    if chunked:
        out_specs, out_shape = y_spec, y_shape
    else:
        s_spec = pl.BlockSpec((N_SEG, HP_W), lambda i, hp: (i, hp))
        s_shape = jax.ShapeDtypeStruct((t // SEG, D_RNN), F32)
        out_specs, out_shape = (y_spec, s_spec, s_spec), (y_shape, s_shape, s_shape)
    scan = pltpu.VMEM((2, N_SLAB, SCAN_ROWS, LANES), F32)
    scratch = [
        pltpu.VMEM((MIX_TT, D_MODEL), BF16),
        pltpu.VMEM((MIX_TT, HP_W), F32),
        pltpu.VMEM((N_SLAB, SCAN_ROWS, LANES), F32),
        scan, scan, scan,
    ] + ([scan] if chunked else [])
    return pl.pallas_call(
        functools.partial(_rg_kernel, chunked=chunked),
        grid=(t // MIX_TT, N_HP),
        in_specs=in_specs,
        out_specs=out_specs,
        out_shape=out_shape,
        scratch_shapes=scratch,
        compiler_params=pltpu.CompilerParams(
            dimension_semantics=("parallel", "arbitrary"), vmem_limit_bytes=VMEM_LIMIT),
        name="rg_scan",
    )(x, mod, norm_g, h0f, h0b, w_in, w_in, conv_w, conv_b, w_gate, b_a, b_x, lam)


def _block_diag_pairs(w):
    n = w.shape[0]
    w = w.reshape(n, 2, N_HP, 2, RNN_HEAD_DIM, RNN_HEAD_DIM)
    z = jnp.zeros_like(w[:, :, :, 0])
    top = jnp.concatenate([w[:, :, :, 0], z], axis=-1)
    bot = jnp.concatenate([z, w[:, :, :, 1]], axis=-1)
    return jnp.concatenate([top, bot], axis=-2)


def _axis_window_sum(v, half, unit, pos, length):
    rows = v.shape[0]

    def down(w, s):
        return jnp.where(pos >= s, pltpu.roll(w, s * unit, axis=0), 0.0)

    def up(w, s):
        return jnp.where(pos < length - s, pltpu.roll(w, rows - s * unit, axis=0), 0.0)

    fwd, bwd, s = v, v, 1
    while s < half:
        fwd = fwd + up(fwd, s)
        bwd = bwd + down(bwd, s)
        s *= 2
    return fwd + down(bwd, 1)


def _window_count(half, pos, length):
    return (jnp.minimum(pos + half, length) - jnp.maximum(pos - half, 0)).astype(F32)


def _pool_kernel(x_ref, mod_ref, ng_ref, pw_ref, ps_ref, o_ref, inv_ref, *, on_grid):
    g = pl.program_id(1)

    @pl.when(g == 0)
    def _():
        x = x_ref[...]
        inv_ref[...] = lax.rsqrt(jnp.mean(x * x, axis=-1, keepdims=True) + EPS)

    row_id = lax.broadcasted_iota(jnp.int32, (MIX_TT, 1), 0)
    for gi, w in enumerate(POOL_WINDOWS):
        @pl.when(g == gi)
        def _(gi=gi, w=w):
            lanes = slice(gi * POOL_GROUP_DIM, (gi + 1) * POOL_GROUP_DIM)
            xg = x_ref[:, lanes] * inv_ref[...] * ng_ref[2:3, lanes]
            xg = xg * (1.0 + mod_ref[4:5, lanes]) + mod_ref[3:4, lanes]
            half = w // 2
            if on_grid:
                col = row_id % GRID_W
                n_rows = MIX_TT // GRID_W
                img_row = row_id // GRID_W
                tot = _axis_window_sum(xg, half, 1, col, GRID_W)
                tot = _axis_window_sum(tot, half, GRID_W, img_row, n_rows)
                cnt = _window_count(half, col, GRID_W) * _window_count(half, img_row, n_rows)
            else:
                pos = row_id % SEG
                tot = _axis_window_sum(xg, half, 1, pos, SEG)
                cnt = _window_count(half, pos, SEG)
            diff = tot / cnt - xg
            o_ref[...] = _dot(diff.astype(BF16), pw_ref[...].astype(BF16)) * ps_ref[...]


def _pool_mix(x, mod, norm_g, pool_w, pool_scale, *, layer, j, on_grid, row0):
    t = x.shape[0]
    row = _mod_row_map(1 if on_grid else None, row0)
    return pl.pallas_call(
        functools.partial(_pool_kernel, on_grid=on_grid),
        grid=(t // MIX_TT, POOL_GROUPS),
        in_specs=[
            pl.BlockSpec((MIX_TT, D_MODEL), lambda i, g: (i, 0)),
            pl.BlockSpec((None, None, N_MOD, D_MODEL), lambda i, g: (layer, row(i), 0, 0)),
            pl.BlockSpec((None, 6, D_MODEL), lambda i, g: (layer, 0, 0)),
            pl.BlockSpec((None, None, POOL_GROUP_DIM, POOL_GROUP_DIM), lambda i, g: (j, g, 0, 0)),
            pl.BlockSpec((None, 1, POOL_GROUP_DIM), lambda i, g: (j, 0, g)),
        ],
        out_specs=pl.BlockSpec((MIX_TT, POOL_GROUP_DIM), lambda i, g: (i, g)),
        out_shape=jax.ShapeDtypeStruct((t, D_MODEL), F32),
        scratch_shapes=[pltpu.VMEM((MIX_TT, 1), F32)],
        compiler_params=pltpu.CompilerParams(
            dimension_semantics=("parallel", "arbitrary"), vmem_limit_bytes=VMEM_LIMIT),
        name="pool_mix",
    )(x, mod, norm_g, pool_w, pool_scale)


def kernel(x_prompt, x_sample, state_rglru, c, c_ctx, mod_w, mod_b, norm_g, ffn_w1, ffn_w3, ffn_w2,
           rg_w_in, rg_conv_w, rg_conv_b, rg_w_a, rg_b_a, rg_w_x, rg_b_x, rg_lam, rg_w_out,
           pool_w, pool_scale):
    batch, seq, _ = x_prompt.shape
    dec_batch, dec_seq, _ = x_sample.shape
    assert seq == SEG and dec_seq == MIX_TT and dec_seq % GRID_W == 0
    assert 1 + dec_batch <= MOD_ROWS
    n_rg = rg_w_in.shape[0]

    cond = jnp.zeros((MOD_ROWS, D_MODEL), F32).at[0].set(c_ctx).at[1:1 + dec_batch].set(c)
    mod = _modulation(cond, mod_w, mod_b).reshape(DEPTH, MOD_ROWS, N_MOD, D_MODEL)

    wa, wx = _block_diag_pairs(rg_w_a), _block_diag_pairs(rg_w_x)
    w_gate = (0.5 * jnp.concatenate([wa[:, 0], wx[:, 0], wa[:, 1], wx[:, 1]], axis=-1)).astype(BF16)
    rg = (rg_w_in, rg_conv_w, rg_conv_b.reshape(n_rg, 1, D_RNN), w_gate, rg_b_a, rg_b_x, rg_lam)
    pool_scale3 = pool_scale.reshape(pool_scale.shape[0], 1, D_MODEL)
    zeros = jnp.zeros((batch, D_RNN), F32)

    xc = x_prompt.reshape(batch * seq, D_MODEL)
    xs = x_sample.reshape(dec_batch * dec_seq, D_MODEL)
    groups = (
        dict(tiles_per_row=None, row0=0),
        dict(tiles_per_row=dec_seq // FFN_TM, row0=1),
    )
    ctx_states = []
    for l in range(DEPTH):
        j = l // 2
        xc = _ffn(xc, mod, norm_g, ffn_w1, ffn_w3, ffn_w2, layer=l, stage=0, **groups[0])
        xs = _ffn(xs, mod, norm_g, ffn_w1, ffn_w3, ffn_w2, layer=l, stage=0, **groups[1])
        if l % 2 == 0:
            yc, sf, sb = _rg_scan(xc, mod, norm_g, zeros, zeros, rg,
                                  layer=l, j=j, seq_len=seq, row0=0)
            ctx_states.append(jnp.stack([sf, sb], axis=1))
            ys = _rg_scan(xs, mod, norm_g, state_rglru[:, j, 0], state_rglru[:, j, 1], rg,
                          layer=l, j=j, seq_len=dec_seq, row0=1)
            w_out = rg_w_out
        else:
            yc = _pool_mix(xc, mod, norm_g, pool_w, pool_scale3, layer=l, j=j, on_grid=False, row0=0)
            ys = _pool_mix(xs, mod, norm_g, pool_w, pool_scale3, layer=l, j=j, on_grid=True, row0=1)
            w_out = None
        xc = _resnorm(xc, yc, mod, norm_g, w_out, layer=l, mixer_idx=j, **groups[0])
        xs = _resnorm(xs, ys, mod, norm_g, w_out, layer=l, mixer_idx=j, **groups[1])
        xc = _ffn(xc, mod, norm_g, ffn_w1, ffn_w3, ffn_w2, layer=l, stage=2, **groups[0])
        xs = _ffn(xs, mod, norm_g, ffn_w1, ffn_w3, ffn_w2, layer=l, stage=2, **groups[1])
    new_state = jnp.stack(ctx_states, axis=1)
    return (xc.reshape(batch, seq, D_MODEL), xs.reshape(dec_batch, dec_seq, D_MODEL), new_state)
```

```python
import functools

import jax
import jax.numpy as jnp
from jax import lax
from jax.experimental import pallas as pl
from jax.experimental.pallas import tpu as pltpu

D_MODEL = 1024
DEPTH = 4
D_RNN = 1024
RNN_HEADS = 8
RNN_HEAD_DIM = D_RNN // RNN_HEADS
CONV_W = 4
RG_C = 8.0
GRID_W = 64
POOL_WINDOWS = (2, 4, 8, 16)
POOL_GROUPS = 4
POOL_GROUP_DIM = D_MODEL // POOL_GROUPS
D_FF = 2816
N_MOD = 9
EPS = 1e-6

F32 = jnp.float32
BF16 = jnp.bfloat16

MOD_ROWS = 8
FFN_TM = 1024
FFN_TF = 256
FFN_TN = 256
FFN_RB = 256
FFN_NA = D_FF // FFN_TF
FFN_NB = D_MODEL // FFN_TN
RES_TM = 1024
SEG = 256
N_SEG = 8
MIX_TT = SEG * N_SEG
SEG_GAP = 4
SEG_PITCH = SEG + SEG_GAP
SCAN_ROWS = SEG_GAP + N_SEG * SEG_PITCH + 4
RG_CHUNK = 2 * SEG
HP_W = 2 * RNN_HEAD_DIM
N_HP = D_RNN // HP_W
LANES = 128
N_SLAB = HP_W // LANES
LOG2_E = 1.4426950408889634
VMEM_LIMIT = 60 * 1024 * 1024


def _dot(a, b):
    return jnp.dot(a, b, preferred_element_type=F32)


def _rms(x, g):
    ms = jnp.mean(x * x, axis=-1, keepdims=True)
    return x * lax.rsqrt(ms + EPS) * g


def _sigmoid(x):
    return 0.5 * jnp.tanh(0.5 * x) + 0.5


def _norm_modulate(x, ng_ref, mod_ref, stage):
    g = ng_ref[2 * stage:2 * stage + 1, :]
    shift = mod_ref[3 * stage:3 * stage + 1, :]
    scale = mod_ref[3 * stage + 1:3 * stage + 2, :]
    inv = lax.rsqrt(jnp.mean(x * x, axis=-1, keepdims=True) + EPS)
    return (x * inv) * (g * (1.0 + scale)) + shift


def _mod_row_map(tiles_per_row, row0):
    if tiles_per_row is None:
        return lambda i: row0
    return lambda i: row0 + i // tiles_per_row


def _mod_kernel(c_ref, w_ref, b_ref, o_ref):
    c = c_ref[...]
    s = c * _sigmoid(c)
    o_ref[...] = _dot(s.astype(BF16), w_ref[...].astype(BF16)) + b_ref[...]


def _modulation(cond, mod_w, mod_b):
    tn = D_MODEL
    n_col = (N_MOD * D_MODEL) // tn
    return pl.pallas_call(
        _mod_kernel,
        grid=(DEPTH, n_col),
        in_specs=[
            pl.BlockSpec((MOD_ROWS, D_MODEL), lambda l, n: (0, 0)),
            pl.BlockSpec((None, D_MODEL, tn), lambda l, n: (l, 0, n)),
            pl.BlockSpec((None, 1, tn), lambda l, n: (l, 0, n)),
        ],
        out_specs=pl.BlockSpec((None, MOD_ROWS, tn), lambda l, n: (l, 0, n)),
        out_shape=jax.ShapeDtypeStruct((DEPTH, MOD_ROWS, N_MOD * D_MODEL), F32),
        compiler_params=pltpu.CompilerParams(
            dimension_semantics=("arbitrary", "arbitrary"), vmem_limit_bytes=VMEM_LIMIT),
        name="modulation",
    )(cond, mod_w, mod_b.reshape(DEPTH, 1, N_MOD * D_MODEL))


def _ffn_kernel(x_ref, xn_ref, mod_ref, modn_ref, ng_ref, w1_ref, w3_ref, w2_ref, o_ref, hn_ref, p_ref,
                *, stage):
    i = pl.program_id(0)
    slot = i % 2

    @pl.when(i == 0)
    def _():
        hn_ref[0] = _norm_modulate(x_ref[...], ng_ref, mod_ref, stage).astype(BF16)

    gain = ng_ref[2 * stage + 1:2 * stage + 2, :] * (0.5 * mod_ref[3 * stage + 2:3 * stage + 3, :])
    for r in range(FFN_TM // FFN_RB):
        rows = slice(r * FFN_RB, (r + 1) * FFN_RB)
        hn = hn_ref[slot, rows, :]
        for k in range(FFN_NA):
            cols = slice(k * FFN_TF, (k + 1) * FFN_TF)
            a = _dot(hn, w1_ref[:, cols])
            b = _dot(hn, w3_ref[:, cols])
            p_ref[k, rows, :] = (a * _sigmoid(a) * b).astype(BF16)
        accs = []
        for n in range(FFN_NB):
            cols = slice(n * FFN_TN, (n + 1) * FFN_TN)
            acc = _dot(p_ref[0, rows, :], w2_ref[0:FFN_TF, cols])
            for k in range(1, FFN_NA):
                acc += _dot(p_ref[k, rows, :], w2_ref[k * FFN_TF:(k + 1) * FFN_TF, cols])
            accs.append(acc)
        ss = sum(jnp.sum(acc * acc, axis=-1, keepdims=True) for acc in accs)
        inv = lax.rsqrt(ss * (1.0 / D_MODEL) + EPS)
        for n, acc in enumerate(accs):
            cols = slice(n * FFN_TN, (n + 1) * FFN_TN)
            o_ref[rows, cols] = x_ref[rows, cols] + (acc * inv) * gain[:, cols]

    hn_ref[1 - slot] = _norm_modulate(xn_ref[...], ng_ref, modn_ref, stage).astype(BF16)


def _ffn(x, mod, norm_g, w1, w3, w2, *, layer, stage, tiles_per_row, row0):
    t = x.shape[0]
    n_tiles = t // FFN_TM
    sub = 0 if stage == 0 else 1
    row = _mod_row_map(tiles_per_row, row0)
    nxt = lambda i: jnp.minimum(i + 1, n_tiles - 1)
    resident = dict(pipeline_mode=pl.Buffered(1))
    return pl.pallas_call(
        functools.partial(_ffn_kernel, stage=stage),
        grid=(n_tiles,),
        in_specs=[
            pl.BlockSpec((FFN_TM, D_MODEL), lambda i: (i, 0)),
            pl.BlockSpec((FFN_TM, D_MODEL), lambda i: (nxt(i), 0)),
            pl.BlockSpec((None, None, N_MOD, D_MODEL), lambda i: (layer, row(i), 0, 0)),
            pl.BlockSpec((None, None, N_MOD, D_MODEL), lambda i: (layer, row(nxt(i)), 0, 0)),
            pl.BlockSpec((None, 6, D_MODEL), lambda i: (layer, 0, 0)),
            pl.BlockSpec((None, None, D_MODEL, D_FF), lambda i: (layer, sub, 0, 0), **resident),
            pl.BlockSpec((None, None, D_MODEL, D_FF), lambda i: (layer, sub, 0, 0), **resident),
            pl.BlockSpec((None, None, D_FF, D_MODEL), lambda i: (layer, sub, 0, 0), **resident),
        ],
        out_specs=pl.BlockSpec((FFN_TM, D_MODEL), lambda i: (i, 0)),
        out_shape=jax.ShapeDtypeStruct(x.shape, F32),
        scratch_shapes=[
            pltpu.VMEM((2, FFN_TM, D_MODEL), BF16),
            pltpu.VMEM((FFN_NA, FFN_TM, FFN_TF), BF16),
        ],
        compiler_params=pltpu.CompilerParams(
            dimension_semantics=("arbitrary",), vmem_limit_bytes=VMEM_LIMIT),
        name="ffn",
    )(x, x, mod, mod, norm_g, w1, w3, w2)


def _resnorm_kernel(x_ref, y_ref, mod_ref, ng_ref, *rest, project):
    if project:
        w_ref, o_ref = rest
        y = _dot(y_ref[...], w_ref[...].astype(BF16))
    else:
        (o_ref,) = rest
        y = y_ref[...]
    gate = mod_ref[5:6, :]
    o_ref[...] = x_ref[...] + gate * _rms(y, ng_ref[3:4, :])


def _resnorm(x, y, mod, norm_g, w_out, *, layer, tiles_per_row, row0, mixer_idx=0):
    t = x.shape[0]
    project = w_out is not None
    row = _mod_row_map(tiles_per_row, row0)
    in_specs = [
        pl.BlockSpec((RES_TM, D_MODEL), lambda i: (i, 0)),
        pl.BlockSpec((RES_TM, D_MODEL), lambda i: (i, 0)),
        pl.BlockSpec((None, None, N_MOD, D_MODEL), lambda i: (layer, row(i), 0, 0)),
        pl.BlockSpec((None, 6, D_MODEL), lambda i: (layer, 0, 0)),
    ]
    args = [x, y, mod, norm_g]
    if project:
        in_specs.append(pl.BlockSpec((None, D_RNN, D_MODEL), lambda i: (mixer_idx, 0, 0)))
        args.append(w_out)
    return pl.pallas_call(
        functools.partial(_resnorm_kernel, project=project),
        grid=(t // RES_TM,),
        in_specs=in_specs,
        out_specs=pl.BlockSpec((RES_TM, D_MODEL), lambda i: (i, 0)),
        out_shape=jax.ShapeDtypeStruct(x.shape, F32),
        compiler_params=pltpu.CompilerParams(
            dimension_semantics=("parallel",), vmem_limit_bytes=VMEM_LIMIT),
        name="resnorm",
    )(*args)


def _seg_base(k):
    return SEG_GAP + k * SEG_PITCH


def _seg_rows(k, offset=0):
    return pl.ds(_seg_base(k) + offset, SEG)


def _rg_kernel(x_ref, mod_ref, ng_ref, h0f_ref, h0b_ref, wig_ref, wix_ref, cw_ref, cb_ref, wg_ref,
               ba_ref, bx_ref, lam_ref, *rest, chunked):
    if chunked:
        y_ref, hn_ref, gg_ref, xr_ref, a_ref, u_ref, h_ref, p_ref = rest
    else:
        y_ref, sf_ref, sb_ref, hn_ref, gg_ref, xr_ref, a_ref, u_ref, h_ref = rest
    hp = pl.program_id(1)

    @pl.when(hp == 0)
    def _():
        hn_ref[...] = _norm_modulate(x_ref[...], ng_ref, mod_ref, 1).astype(BF16)

    wig = wig_ref[...].astype(BF16)
    wix = wix_ref[...].astype(BF16)
    seg_per_chunk = RG_CHUNK // SEG
    zero2 = jnp.zeros((2, LANES), F32)

    def project(c):
        rows = slice(c * RG_CHUNK, (c + 1) * RG_CHUNK)
        hn = hn_ref[rows, :]
        gg_ref[rows, :] = jax.nn.gelu(_dot(hn, wig))
        xr = _dot(hn, wix)
        for s in range(N_SLAB):
            lanes = slice(s * LANES, (s + 1) * LANES)
            for kk in range(seg_per_chunk):
                k = c * seg_per_chunk + kk
                base = _seg_base(k)
                seg = xr[kk * SEG:(kk + 1) * SEG, lanes]
                xr_ref[s, _seg_rows(k), :] = seg
                inside = chunked and k > 0
                xr_ref[s, base - SEG_GAP:base - 2, :] = seg[0:2, :] if inside else zero2
                inside = chunked and k < N_SEG - 1
                xr_ref[s, base + SEG + 2:base + SEG + SEG_GAP, :] = seg[SEG - 2:SEG, :] if inside else zero2
                if k == 0:
                    xr_ref[s, base - 2:base, :] = zero2
                if k == N_SEG - 1:
                    xr_ref[s, base + SEG:base + SEG + 2, :] = zero2

    z = -lam_ref[...]
    softplus = jnp.maximum(z, 0.0) + jnp.log1p(jnp.exp(-jnp.abs(z)))
    half_rate = (0.5 * (-RG_C)) * softplus
    half_ba = 0.5 * ba_ref[...]
    half_bx = 0.5 * bx_ref[...]
    cw = cw_ref[...]
    cb = cb_ref[...]

    def gates(k):
        taps = []
        for s in range(N_SLAB):
            lanes = slice(s * LANES, (s + 1) * LANES)
            acc = cb[:, lanes] + cw[0:1, lanes] * xr_ref[s, _seg_rows(k, -1), :]
            for j in range(1, CONV_W):
                acc += cw[j:j + 1, lanes] * xr_ref[s, _seg_rows(k, j - 1), :]
            taps.append(acc)
        xk = jnp.concatenate(taps, axis=1)
        pre = _dot(xk.astype(BF16), wg_ref[...])
        xh = 0.5 * xk
        for d in range(2):
            tr = jnp.tanh(pre[:, (2 * d) * HP_W:(2 * d + 1) * HP_W] + half_ba[d:d + 1, :])
            ti = jnp.tanh(pre[:, (2 * d + 1) * HP_W:(2 * d + 2) * HP_W] + half_bx[d:d + 1, :])
            a = jnp.exp(tr * half_rate[d:d + 1, :] + half_rate[d:d + 1, :])
            v = 1.0 - a * a
            beta = jnp.where(v > 0.0, v * lax.rsqrt(v), 0.0)
            u = beta * (xh * ti + xh)
            for s in range(N_SLAB):
                lanes = slice(s * LANES, (s + 1) * LANES)
                a_ref[d, s, _seg_rows(k), :] = a[:, lanes]
                u_ref[d, s, _seg_rows(k), :] = u[:, lanes]

    n_chunk = MIX_TT // RG_CHUNK
    project(0)
    for c in range(n_chunk):
        if c + 1 < n_chunk:
            project(c + 1)
        first = c * seg_per_chunk - 1
        for k in range(max(first, 0), first + seg_per_chunk):
            gates(k)
    gates(N_SEG - 1)

    def step(t, carry):
        hs, ps = carry
        new_h, new_p = [], []
        for d in range(2):
            row = SEG_GAP + (t if d == 0 else SEG - 1 - t)
            idx = pl.ds(row, N_SEG, stride=SEG_PITCH)
            for s in range(N_SLAB):
                a = a_ref[d, s, idx, :]
                h = a * hs[d * N_SLAB + s] + u_ref[d, s, idx, :]
                h_ref[d, s, idx, :] = h
                new_h.append(h)
                if chunked:
                    p = a * ps[d * N_SLAB + s]
                    p_ref[d, s, idx, :] = p
                    new_p.append(p)
        return tuple(new_h), tuple(new_p)

    if chunked:
        h_init = tuple(jnp.zeros((N_SEG, LANES), F32) for _ in range(2 * N_SLAB))
        p_init = tuple(jnp.ones((N_SEG, LANES), F32) for _ in range(2 * N_SLAB))
    else:
        h0 = (h0f_ref[...], h0b_ref[...])
        h_init = tuple(h0[d][:, s * LANES:(s + 1) * LANES] for d in range(2) for s in range(N_SLAB))
        p_init = ()
    h_fin, _ = lax.fori_loop(0, SEG, step, (h_init, p_init), unroll=8)

    if chunked:
        b = pl.program_id(0)
        h0 = (h0f_ref[pl.ds(b, 1), :], h0b_ref[pl.ds(b, 1), :])
        for s in range(N_SLAB):
            lanes = slice(s * LANES, (s + 1) * LANES)
            carry = h0[0][:, lanes]
            for k in range(N_SEG):
                h = h_ref[0, s, _seg_rows(k), :] + p_ref[0, s, _seg_rows(k), :] * carry
                h_ref[0, s, _seg_rows(k), :] = h
                carry = h[SEG - 1:SEG, :]
            carry = h0[1][:, lanes]
            for k in reversed(range(N_SEG)):
                hb = h_ref[1, s, _seg_rows(k), :] + p_ref[1, s, _seg_rows(k), :] * carry
                carry = hb[0:1, :]
                rows = slice(k * SEG, (k + 1) * SEG)
                y = (h_ref[0, s, _seg_rows(k), :] + hb) * gg_ref[rows, lanes]
                y_ref[rows, lanes] = y.astype(BF16)
    else:
        for s in range(N_SLAB):
            lanes = slice(s * LANES, (s + 1) * LANES)
            sf_ref[:, lanes] = h_fin[s]
            sb_ref[:, lanes] = h_fin[N_SLAB + s]
            for k in range(N_SEG):
                rows = slice(k * SEG, (k + 1) * SEG)
                y = (h_ref[0, s, _seg_rows(k), :] + h_ref[1, s, _seg_rows(k), :]) * gg_ref[rows, lanes]
                y_ref[rows, lanes] = y.astype(BF16)


def _rg_scan(x, mod, norm_g, h0f, h0b, rg, *, layer, j, seq_len, row0):
    t = x.shape[0]
    chunked = seq_len > SEG
    assert seq_len == (MIX_TT if chunked else SEG)
    w_in, conv_w, conv_b, w_gate, b_a, b_x, lam = rg
    row = _mod_row_map(1 if chunked else None, row0)
    n_state = h0f.shape[0]
    state_block = (n_state, HP_W) if chunked else (N_SEG, HP_W)
    state_map = (lambda i, hp: (0, hp)) if chunked else (lambda i, hp: (i, hp))
    in_specs = [
        pl.BlockSpec((MIX_TT, D_MODEL), lambda i, hp: (i, 0)),
        pl.BlockSpec((None, None, N_MOD, D_MODEL), lambda i, hp: (layer, row(i), 0, 0)),
        pl.BlockSpec((None, 6, D_MODEL), lambda i, hp: (layer, 0, 0)),
        pl.BlockSpec(state_block, state_map),
        pl.BlockSpec(state_block, state_map),
        pl.BlockSpec((None, D_MODEL, HP_W), lambda i, hp: (j, 0, hp)),
        pl.BlockSpec((None, D_MODEL, HP_W), lambda i, hp: (j, 0, N_HP + hp)),
        pl.BlockSpec((None, CONV_W, HP_W), lambda i, hp: (j, 0, hp)),
        pl.BlockSpec((None, 1, HP_W), lambda i, hp: (j, 0, hp)),
        pl.BlockSpec((None, None, HP_W, 4 * HP_W), lambda i, hp: (j, hp, 0, 0)),
        pl.BlockSpec((None, 2, HP_W), lambda i, hp: (j, 0, hp)),
        pl.BlockSpec((None, 2, HP_W), lambda i, hp: (j, 0, hp)),
        pl.BlockSpec((None, 2, HP_W), lambda i, hp: (j, 0, hp)),
    ]
    y_spec = pl.BlockSpec((MIX_TT, HP_W), lambda i, hp: (i, hp))
    y_shape = jax.ShapeDtypeStruct((t, D_RNN), BF16)
    if chunked:
        out_specs, out_shape = y_spec, y_shape
    else:
        s_spec = pl.BlockSpec((N_SEG, HP_W), lambda i, hp: (i, hp))
        s_shape = jax.ShapeDtypeStruct((t // SEG, D_RNN), F32)
        out_specs, out_shape = (y_spec, s_spec, s_spec), (y_shape, s_shape, s_shape)
    scan = pltpu.VMEM((2, N_SLAB, SCAN_ROWS, LANES), F32)
    scratch = [
        pltpu.VMEM((MIX_TT, D_MODEL), BF16),
        pltpu.VMEM((MIX_TT, HP_W), F32),
        pltpu.VMEM((N_SLAB, SCAN_ROWS, LANES), F32),
        scan, scan, scan,
    ] + ([scan] if chunked else [])
    return pl.pallas_call(
        functools.partial(_rg_kernel, chunked=chunked),
        grid=(t // MIX_TT, N_HP),
        in_specs=in_specs,
        out_specs=out_specs,
        out_shape=out_shape,
        scratch_shapes=scratch,
        compiler_params=pltpu.CompilerParams(
            dimension_semantics=("parallel", "arbitrary"), vmem_limit_bytes=VMEM_LIMIT),
        name="rg_scan",
    )(x, mod, norm_g, h0f, h0b, w_in, w_in, conv_w, conv_b, w_gate, b_a, b_x, lam)


def _block_diag_pairs(w):
    n = w.shape[0]
    w = w.reshape(n, 2, N_HP, 2, RNN_HEAD_DIM, RNN_HEAD_DIM)
    z = jnp.zeros_like(w[:, :, :, 0])
    top = jnp.concatenate([w[:, :, :, 0], z], axis=-1)
    bot = jnp.concatenate([z, w[:, :, :, 1]], axis=-1)
    return jnp.concatenate([top, bot], axis=-2)


def _axis_window_sum(v, half, unit, pos, length):
    rows = v.shape[0]

    def down(w, s):
        return jnp.where(pos >= s, pltpu.roll(w, s * unit, axis=0), 0.0)

    def up(w, s):
        return jnp.where(pos < length - s, pltpu.roll(w, rows - s * unit, axis=0), 0.0)

    fwd, bwd, s = v, v, 1
    while s < half:
        fwd = fwd + up(fwd, s)
        bwd = bwd + down(bwd, s)
        s *= 2
    return fwd + down(bwd, 1)


def _window_count(half, pos, length):
    return (jnp.minimum(pos + half, length) - jnp.maximum(pos - half, 0)).astype(F32)


def _pool_kernel(x_ref, mod_ref, ng_ref, pw_ref, ps_ref, o_ref, inv_ref, *, on_grid):
    g = pl.program_id(1)

    @pl.when(g == 0)
    def _():
        x = x_ref[...]
        inv_ref[...] = lax.rsqrt(jnp.mean(x * x, axis=-1, keepdims=True) + EPS)

    row_id = lax.broadcasted_iota(jnp.int32, (MIX_TT, 1), 0)
    for gi, w in enumerate(POOL_WINDOWS):
        @pl.when(g == gi)
        def _(gi=gi, w=w):
            lanes = slice(gi * POOL_GROUP_DIM, (gi + 1) * POOL_GROUP_DIM)
            xg = x_ref[:, lanes] * inv_ref[...] * ng_ref[2:3, lanes]
            xg = xg * (1.0 + mod_ref[4:5, lanes]) + mod_ref[3:4, lanes]
            half = w // 2
            if on_grid:
                col = row_id % GRID_W
                n_rows = MIX_TT // GRID_W
                img_row = row_id // GRID_W
                tot = _axis_window_sum(xg, half, 1, col, GRID_W)
                tot = _axis_window_sum(tot, half, GRID_W, img_row, n_rows)
                cnt = _window_count(half, col, GRID_W) * _window_count(half, img_row, n_rows)
            else:
                pos = row_id % SEG
                tot = _axis_window_sum(xg, half, 1, pos, SEG)
                cnt = _window_count(half, pos, SEG)
            diff = tot / cnt - xg
            o_ref[...] = _dot(diff.astype(BF16), pw_ref[...].astype(BF16)) * ps_ref[...]


def _pool_mix(x, mod, norm_g, pool_w, pool_scale, *, layer, j, on_grid, row0):
    t = x.shape[0]
    row = _mod_row_map(1 if on_grid else None, row0)
    return pl.pallas_call(
        functools.partial(_pool_kernel, on_grid=on_grid),
        grid=(t // MIX_TT, POOL_GROUPS),
        in_specs=[
            pl.BlockSpec((MIX_TT, D_MODEL), lambda i, g: (i, 0)),
            pl.BlockSpec((None, None, N_MOD, D_MODEL), lambda i, g: (layer, row(i), 0, 0)),
            pl.BlockSpec((None, 6, D_MODEL), lambda i, g: (layer, 0, 0)),
            pl.BlockSpec((None, None, POOL_GROUP_DIM, POOL_GROUP_DIM), lambda i, g: (j, g, 0, 0)),
            pl.BlockSpec((None, 1, POOL_GROUP_DIM), lambda i, g: (j, 0, g)),
        ],
        out_specs=pl.BlockSpec((MIX_TT, POOL_GROUP_DIM), lambda i, g: (i, g)),
        out_shape=jax.ShapeDtypeStruct((t, D_MODEL), F32),
        scratch_shapes=[pltpu.VMEM((MIX_TT, 1), F32)],
        compiler_params=pltpu.CompilerParams(
            dimension_semantics=("parallel", "arbitrary"), vmem_limit_bytes=VMEM_LIMIT),
        name="pool_mix",
    )(x, mod, norm_g, pool_w, pool_scale)


def kernel(x_prompt, x_sample, state_rglru, c, c_ctx, mod_w, mod_b, norm_g, ffn_w1, ffn_w3, ffn_w2,
           rg_w_in, rg_conv_w, rg_conv_b, rg_w_a, rg_b_a, rg_w_x, rg_b_x, rg_lam, rg_w_out,
           pool_w, pool_scale):
    batch, seq, _ = x_prompt.shape
    dec_batch, dec_seq, _ = x_sample.shape
    assert seq == SEG and dec_seq == MIX_TT and dec_seq % GRID_W == 0
    assert 1 + dec_batch <= MOD_ROWS
    n_rg = rg_w_in.shape[0]

    cond = jnp.zeros((MOD_ROWS, D_MODEL), F32).at[0].set(c_ctx).at[1:1 + dec_batch].set(c)
    mod = _modulation(cond, mod_w, mod_b).reshape(DEPTH, MOD_ROWS, N_MOD, D_MODEL)

    wa, wx = _block_diag_pairs(rg_w_a), _block_diag_pairs(rg_w_x)
    w_gate = (0.5 * jnp.concatenate([wa[:, 0], wx[:, 0], wa[:, 1], wx[:, 1]], axis=-1)).astype(BF16)
    rg = (rg_w_in, rg_conv_w, rg_conv_b.reshape(n_rg, 1, D_RNN), w_gate, rg_b_a, rg_b_x, rg_lam)
    pool_scale3 = pool_scale.reshape(pool_scale.shape[0], 1, D_MODEL)
    zeros = jnp.zeros((batch, D_RNN), F32)
    ffn_w1, ffn_w3, ffn_w2 = (w.astype(BF16) for w in (ffn_w1, ffn_w3, ffn_w2))

    xc = x_prompt.reshape(batch * seq, D_MODEL)
    xs = x_sample.reshape(dec_batch * dec_seq, D_MODEL)
    groups = (
        dict(tiles_per_row=None, row0=0),
        dict(tiles_per_row=dec_seq // FFN_TM, row0=1),
    )
    ctx_states = []
    for l in range(DEPTH):
        j = l // 2
        xc = _ffn(xc, mod, norm_g, ffn_w1, ffn_w3, ffn_w2, layer=l, stage=0, **groups[0])
        xs = _ffn(xs, mod, norm_g, ffn_w1, ffn_w3, ffn_w2, layer=l, stage=0, **groups[1])
        if l % 2 == 0:
            yc, sf, sb = _rg_scan(xc, mod, norm_g, zeros, zeros, rg,
                                  layer=l, j=j, seq_len=seq, row0=0)
            ctx_states.append(jnp.stack([sf, sb], axis=1))
            ys = _rg_scan(xs, mod, norm_g, state_rglru[:, j, 0], state_rglru[:, j, 1], rg,
                          layer=l, j=j, seq_len=dec_seq, row0=1)
            w_out = rg_w_out
        else:
            yc = _pool_mix(xc, mod, norm_g, pool_w, pool_scale3, layer=l, j=j, on_grid=False, row0=0)
            ys = _pool_mix(xs, mod, norm_g, pool_w, pool_scale3, layer=l, j=j, on_grid=True, row0=1)
            w_out = None
        xc = _resnorm(xc, yc, mod, norm_g, w_out, layer=l, mixer_idx=j, **groups[0])
        xs = _resnorm(xs, ys, mod, norm_g, w_out, layer=l, mixer_idx=j, **groups[1])
        xc = _ffn(xc, mod, norm_g, ffn_w1, ffn_w3, ffn_w2, layer=l, stage=2, **groups[0])
        xs = _ffn(xs, mod, norm_g, ffn_w1, ffn_w3, ffn_w2, layer=l, stage=2, **groups[1])
    new_state = jnp.stack(ctx_states, axis=1)
    return (xc.reshape(batch, seq, D_MODEL), xs.reshape(dec_batch, dec_seq, D_MODEL), new_state)
```

```python
import functools

import jax
import jax.numpy as jnp
from jax import lax
from jax.experimental import pallas as pl
from jax.experimental.pallas import tpu as pltpu

D_MODEL = 1024
DEPTH = 4
D_RNN = 1024
RNN_HEADS = 8
RNN_HEAD_DIM = D_RNN // RNN_HEADS
CONV_W = 4
RG_C = 8.0
GRID_W = 64
POOL_WINDOWS = (2, 4, 8, 16)
POOL_GROUPS = 4
POOL_GROUP_DIM = D_MODEL // POOL_GROUPS
D_FF = 2816
N_MOD = 9
EPS = 1e-6

F32 = jnp.float32
BF16 = jnp.bfloat16

MOD_ROWS = 8
FFN_TM = 1024
FFN_TF = 256
FFN_TN = 256
FFN_RB = 256
FFN_NA = D_FF // FFN_TF
FFN_NB = D_MODEL // FFN_TN
RES_TM = 1024
SEG = 256
N_SEG = 8
MIX_TT = SEG * N_SEG
SEG_GAP = 4
SEG_PITCH = SEG + SEG_GAP
SCAN_ROWS = SEG_GAP + N_SEG * SEG_PITCH + 4
RG_CHUNK = 2 * SEG
HP_W = 2 * RNN_HEAD_DIM
N_HP = D_RNN // HP_W
LANES = 128
N_SLAB = HP_W // LANES
POOL_HALO = max(POOL_WINDOWS) // 2
POOL_EDGE = 8
VMEM_LIMIT = 60 * 1024 * 1024


def _dot(a, b):
    return jnp.dot(a, b, preferred_element_type=F32)


def _rms(x, g):
    ms = jnp.mean(x * x, axis=-1, keepdims=True)
    return x * lax.rsqrt(ms + EPS) * g


def _sigmoid(x):
    return 0.5 * jnp.tanh(0.5 * x) + 0.5


def _norm_modulate(x, ng_ref, mod_ref, stage):
    g = ng_ref[2 * stage:2 * stage + 1, :]
    shift = mod_ref[3 * stage:3 * stage + 1, :]
    scale = mod_ref[3 * stage + 1:3 * stage + 2, :]
    inv = lax.rsqrt(jnp.mean(x * x, axis=-1, keepdims=True) + EPS)
    return (x * inv) * (g * (1.0 + scale)) + shift


def _mod_row_map(tiles_per_row, row0):
    if tiles_per_row is None:
        return lambda i: row0
    return lambda i: row0 + i // tiles_per_row


def _mod_kernel(c_ref, w_ref, b_ref, o_ref):
    c = c_ref[...]
    s = c * _sigmoid(c)
    o_ref[...] = _dot(s.astype(BF16), w_ref[...].astype(BF16)) + b_ref[...]


def _modulation(cond, mod_w, mod_b):
    tn = D_MODEL
    n_col = (N_MOD * D_MODEL) // tn
    return pl.pallas_call(
        _mod_kernel,
        grid=(DEPTH, n_col),
        in_specs=[
            pl.BlockSpec((MOD_ROWS, D_MODEL), lambda l, n: (0, 0)),
            pl.BlockSpec((None, D_MODEL, tn), lambda l, n: (l, 0, n)),
            pl.BlockSpec((None, 1, tn), lambda l, n: (l, 0, n)),
        ],
        out_specs=pl.BlockSpec((None, MOD_ROWS, tn), lambda l, n: (l, 0, n)),
        out_shape=jax.ShapeDtypeStruct((DEPTH, MOD_ROWS, N_MOD * D_MODEL), F32),
        compiler_params=pltpu.CompilerParams(
            dimension_semantics=("arbitrary", "arbitrary"), vmem_limit_bytes=VMEM_LIMIT),
        name="modulation",
    )(cond, mod_w, mod_b.reshape(DEPTH, 1, N_MOD * D_MODEL))


def _ffn_kernel(x_ref, xn_ref, mod_ref, modn_ref, ng_ref, w1_ref, w3_ref, w2_ref, o_ref, hn_ref, p_ref,
                *, stage):
    i = pl.program_id(0)
    slot = i % 2

    @pl.when(i == 0)
    def _():
        hn_ref[0] = _norm_modulate(x_ref[...], ng_ref, mod_ref, stage).astype(BF16)

    gain = ng_ref[2 * stage + 1:2 * stage + 2, :] * (0.5 * mod_ref[3 * stage + 2:3 * stage + 3, :])
    for r in range(FFN_TM // FFN_RB):
        rows = slice(r * FFN_RB, (r + 1) * FFN_RB)
        hn = hn_ref[slot, rows, :]
        for k in range(FFN_NA):
            cols = slice(k * FFN_TF, (k + 1) * FFN_TF)
            a = _dot(hn, w1_ref[:, cols])
            b = _dot(hn, w3_ref[:, cols])
            p_ref[k, rows, :] = (a * _sigmoid(a) * b).astype(BF16)
        accs = []
        for n in range(FFN_NB):
            cols = slice(n * FFN_TN, (n + 1) * FFN_TN)
            acc = _dot(p_ref[0, rows, :], w2_ref[0:FFN_TF, cols])
            for k in range(1, FFN_NA):
                acc += _dot(p_ref[k, rows, :], w2_ref[k * FFN_TF:(k + 1) * FFN_TF, cols])
            accs.append(acc)
        ss = sum(jnp.sum(acc * acc, axis=-1, keepdims=True) for acc in accs)
        inv = lax.rsqrt(ss * (1.0 / D_MODEL) + EPS)
        for n, acc in enumerate(accs):
            cols = slice(n * FFN_TN, (n + 1) * FFN_TN)
            o_ref[rows, cols] = x_ref[rows, cols] + (acc * inv) * gain[:, cols]

    hn_ref[1 - slot] = _norm_modulate(xn_ref[...], ng_ref, modn_ref, stage).astype(BF16)


def _ffn(x, mod, norm_g, w1, w3, w2, *, layer, stage, tiles_per_row, row0):
    t = x.shape[0]
    n_tiles = t // FFN_TM
    sub = 0 if stage == 0 else 1
    row = _mod_row_map(tiles_per_row, row0)
    nxt = lambda i: jnp.minimum(i + 1, n_tiles - 1)
    resident = dict(pipeline_mode=pl.Buffered(1))
    return pl.pallas_call(
        functools.partial(_ffn_kernel, stage=stage),
        grid=(n_tiles,),
        in_specs=[
            pl.BlockSpec((FFN_TM, D_MODEL), lambda i: (i, 0)),
            pl.BlockSpec((FFN_TM, D_MODEL), lambda i: (nxt(i), 0)),
            pl.BlockSpec((None, None, N_MOD, D_MODEL), lambda i: (layer, row(i), 0, 0)),
            pl.BlockSpec((None, None, N_MOD, D_MODEL), lambda i: (layer, row(nxt(i)), 0, 0)),
            pl.BlockSpec((None, 6, D_MODEL), lambda i: (layer, 0, 0)),
            pl.BlockSpec((None, None, D_MODEL, D_FF), lambda i: (layer, sub, 0, 0), **resident),
            pl.BlockSpec((None, None, D_MODEL, D_FF), lambda i: (layer, sub, 0, 0), **resident),
            pl.BlockSpec((None, None, D_FF, D_MODEL), lambda i: (layer, sub, 0, 0), **resident),
        ],
        out_specs=pl.BlockSpec((FFN_TM, D_MODEL), lambda i: (i, 0)),
        out_shape=jax.ShapeDtypeStruct(x.shape, F32),
        scratch_shapes=[
            pltpu.VMEM((2, FFN_TM, D_MODEL), BF16),
            pltpu.VMEM((FFN_NA, FFN_TM, FFN_TF), BF16),
        ],
        compiler_params=pltpu.CompilerParams(
            dimension_semantics=("arbitrary",), vmem_limit_bytes=VMEM_LIMIT),
        name="ffn",
    )(x, x, mod, mod, norm_g, w1, w3, w2)


def _out_proj_kernel(x_ref, y_ref, mod_ref, ng_ref, w_ref, o_ref):
    y = _dot(y_ref[...], w_ref[...].astype(BF16))
    o_ref[...] = x_ref[...] + mod_ref[5:6, :] * _rms(y, ng_ref[3:4, :])


def _out_proj(x, y, mod, norm_g, w_out, *, layer, j, tiles_per_row, row0):
    t = x.shape[0]
    row = _mod_row_map(tiles_per_row, row0)
    return pl.pallas_call(
        _out_proj_kernel,
        grid=(t // RES_TM,),
        in_specs=[
            pl.BlockSpec((RES_TM, D_MODEL), lambda i: (i, 0)),
            pl.BlockSpec((RES_TM, D_RNN), lambda i: (i, 0)),
            pl.BlockSpec((None, None, N_MOD, D_MODEL), lambda i: (layer, row(i), 0, 0)),
            pl.BlockSpec((None, 6, D_MODEL), lambda i: (layer, 0, 0)),
            pl.BlockSpec((None, D_RNN, D_MODEL), lambda i: (j, 0, 0)),
        ],
        out_specs=pl.BlockSpec((RES_TM, D_MODEL), lambda i: (i, 0)),
        out_shape=jax.ShapeDtypeStruct(x.shape, F32),
        compiler_params=pltpu.CompilerParams(
            dimension_semantics=("parallel",), vmem_limit_bytes=VMEM_LIMIT),
        name="out_proj",
    )(x, y, mod, norm_g, w_out)


def _seg_base(k):
    return SEG_GAP + k * SEG_PITCH


def _seg_rows(k, offset=0):
    return pl.ds(_seg_base(k) + offset, SEG)


def _rg_kernel(x_ref, mod_ref, ng_ref, h0f_ref, h0b_ref, wig_ref, wix_ref, cw_ref, cb_ref, wg_ref,
               ba_ref, bx_ref, lam_ref, *rest, chunked):
    if chunked:
        y_ref, hn_ref, gg_ref, xr_ref, a_ref, u_ref, h_ref, p_ref = rest
    else:
        y_ref, sf_ref, sb_ref, hn_ref, gg_ref, xr_ref, a_ref, u_ref, h_ref = rest
    hp = pl.program_id(1)

    @pl.when(hp == 0)
    def _():
        hn_ref[...] = _norm_modulate(x_ref[...], ng_ref, mod_ref, 1).astype(BF16)

    wig = wig_ref[...].astype(BF16)
    wix = wix_ref[...].astype(BF16)
    seg_per_chunk = RG_CHUNK // SEG
    zero2 = jnp.zeros((2, LANES), F32)

    def project(c):
        rows = slice(c * RG_CHUNK, (c + 1) * RG_CHUNK)
        hn = hn_ref[rows, :]
        gg_ref[rows, :] = jax.nn.gelu(_dot(hn, wig))
        xr = _dot(hn, wix)
        for s in range(N_SLAB):
            lanes = slice(s * LANES, (s + 1) * LANES)
            for kk in range(seg_per_chunk):
                k = c * seg_per_chunk + kk
                base = _seg_base(k)
                seg = xr[kk * SEG:(kk + 1) * SEG, lanes]
                xr_ref[s, _seg_rows(k), :] = seg
                inside = chunked and k > 0
                xr_ref[s, base - SEG_GAP:base - 2, :] = seg[0:2, :] if inside else zero2
                inside = chunked and k < N_SEG - 1
                xr_ref[s, base + SEG + 2:base + SEG + SEG_GAP, :] = seg[SEG - 2:SEG, :] if inside else zero2
                if k == 0:
                    xr_ref[s, base - 2:base, :] = zero2
                if k == N_SEG - 1:
                    xr_ref[s, base + SEG:base + SEG + 2, :] = zero2

    z = -lam_ref[...]
    softplus = jnp.maximum(z, 0.0) + jnp.log1p(jnp.exp(-jnp.abs(z)))
    half_rate = (0.5 * (-RG_C)) * softplus
    half_ba = 0.5 * ba_ref[...]
    half_bx = 0.5 * bx_ref[...]
    cw = cw_ref[...]
    cb = cb_ref[...]

    def gates(k):
        taps = []
        for s in range(N_SLAB):
            lanes = slice(s * LANES, (s + 1) * LANES)
            acc = cb[:, lanes] + cw[0:1, lanes] * xr_ref[s, _seg_rows(k, -1), :]
            for j in range(1, CONV_W):
                acc += cw[j:j + 1, lanes] * xr_ref[s, _seg_rows(k, j - 1), :]
            taps.append(acc)
        xk = jnp.concatenate(taps, axis=1)
        pre = _dot(xk.astype(BF16), wg_ref[...])
        xh = 0.5 * xk
        for d in range(2):
            tr = jnp.tanh(pre[:, (2 * d) * HP_W:(2 * d + 1) * HP_W] + half_ba[d:d + 1, :])
            ti = jnp.tanh(pre[:, (2 * d + 1) * HP_W:(2 * d + 2) * HP_W] + half_bx[d:d + 1, :])
            a = jnp.exp(tr * half_rate[d:d + 1, :] + half_rate[d:d + 1, :])
            v = 1.0 - a * a
            beta = jnp.where(v > 0.0, v * lax.rsqrt(v), 0.0)
            u = beta * (xh * ti + xh)
            for s in range(N_SLAB):
                lanes = slice(s * LANES, (s + 1) * LANES)
                a_ref[d, s, _seg_rows(k), :] = a[:, lanes]
                u_ref[d, s, _seg_rows(k), :] = u[:, lanes]

    n_chunk = MIX_TT // RG_CHUNK
    project(0)
    for c in range(n_chunk):
        if c + 1 < n_chunk:
            project(c + 1)
        first = c * seg_per_chunk - 1
        for k in range(max(first, 0), first + seg_per_chunk):
            gates(k)
    gates(N_SEG - 1)

    def step(t, carry):
        hs, ps = carry
        new_h, new_p = [], []
        for d in range(2):
            row = SEG_GAP + (t if d == 0 else SEG - 1 - t)
            idx = pl.ds(row, N_SEG, stride=SEG_PITCH)
            for s in range(N_SLAB):
                a = a_ref[d, s, idx, :]
                h = a * hs[d * N_SLAB + s] + u_ref[d, s, idx, :]
                h_ref[d, s, idx, :] = h
                new_h.append(h)
                if chunked:
                    p = a * ps[d * N_SLAB + s]
                    p_ref[d, s, idx, :] = p
                    new_p.append(p)
        return tuple(new_h), tuple(new_p)

    if chunked:
        h_init = tuple(jnp.zeros((N_SEG, LANES), F32) for _ in range(2 * N_SLAB))
        p_init = tuple(jnp.ones((N_SEG, LANES), F32) for _ in range(2 * N_SLAB))
    else:
        h0 = (h0f_ref[...], h0b_ref[...])
        h_init = tuple(h0[d][:, s * LANES:(s + 1) * LANES] for d in range(2) for s in range(N_SLAB))
        p_init = ()
    h_fin, _ = lax.fori_loop(0, SEG, step, (h_init, p_init), unroll=8)

    if chunked:
        b = pl.program_id(0)
        h0 = (h0f_ref[pl.ds(b, 1), :], h0b_ref[pl.ds(b, 1), :])
        for s in range(N_SLAB):
            lanes = slice(s * LANES, (s + 1) * LANES)
            carry = h0[0][:, lanes]
            for k in range(N_SEG):
                h = h_ref[0, s, _seg_rows(k), :] + p_ref[0, s, _seg_rows(k), :] * carry
                h_ref[0, s, _seg_rows(k), :] = h
                carry = h[SEG - 1:SEG, :]
            carry = h0[1][:, lanes]
            for k in reversed(range(N_SEG)):
                hb = h_ref[1, s, _seg_rows(k), :] + p_ref[1, s, _seg_rows(k), :] * carry
                carry = hb[0:1, :]
                rows = slice(k * SEG, (k + 1) * SEG)
                y = (h_ref[0, s, _seg_rows(k), :] + hb) * gg_ref[rows, lanes]
                y_ref[rows, lanes] = y.astype(BF16)
    else:
        for s in range(N_SLAB):
            lanes = slice(s * LANES, (s + 1) * LANES)
            sf_ref[:, lanes] = h_fin[s]
            sb_ref[:, lanes] = h_fin[N_SLAB + s]
            for k in range(N_SEG):
                rows = slice(k * SEG, (k + 1) * SEG)
                y = (h_ref[0, s, _seg_rows(k), :] + h_ref[1, s, _seg_rows(k), :]) * gg_ref[rows, lanes]
                y_ref[rows, lanes] = y.astype(BF16)


def _rg_scan(x, mod, norm_g, h0f, h0b, rg, *, layer, j, seq_len, row0):
    t = x.shape[0]
    chunked = seq_len > SEG
    assert seq_len == (MIX_TT if chunked else SEG)
    w_in, conv_w, conv_b, w_gate, b_a, b_x, lam = rg
    row = _mod_row_map(1 if chunked else None, row0)
    n_state = h0f.shape[0]
    state_block = (n_state, HP_W) if chunked else (N_SEG, HP_W)
    state_map = (lambda i, hp: (0, hp)) if chunked else (lambda i, hp: (i, hp))
    in_specs = [
        pl.BlockSpec((MIX_TT, D_MODEL), lambda i, hp: (i, 0)),
        pl.BlockSpec((None, None, N_MOD, D_MODEL), lambda i, hp: (layer, row(i), 0, 0)),
        pl.BlockSpec((None, 6, D_MODEL), lambda i, hp: (layer, 0, 0)),
        pl.BlockSpec(state_block, state_map),
        pl.BlockSpec(state_block, state_map),
        pl.BlockSpec((None, D_MODEL, HP_W), lambda i, hp: (j, 0, hp)),
        pl.BlockSpec((None, D_MODEL, HP_W), lambda i, hp: (j, 0, N_HP + hp)),
        pl.BlockSpec((None, CONV_W, HP_W), lambda i, hp: (j, 0, hp)),
        pl.BlockSpec((None, 1, HP_W), lambda i, hp: (j, 0, hp)),
        pl.BlockSpec((None, None, HP_W, 4 * HP_W), lambda i, hp: (j, hp, 0, 0)),
        pl.BlockSpec((None, 2, HP_W), lambda i, hp: (j, 0, hp)),
        pl.BlockSpec((None, 2, HP_W), lambda i, hp: (j, 0, hp)),
        pl.BlockSpec((None, 2, HP_W), lambda i, hp: (j, 0, hp)),
    ]
    y_spec = pl.BlockSpec((MIX_TT, HP_W), lambda i, hp: (i, hp))
    y_shape = jax.ShapeDtypeStruct((t, D_RNN), BF16)
    if chunked:
        out_specs, out_shape = y_spec, y_shape
    else:
        s_spec = pl.BlockSpec((N_SEG, HP_W), lambda i, hp: (i, hp))
        s_shape = jax.ShapeDtypeStruct((t // SEG, D_RNN), F32)
        out_specs, out_shape = (y_spec, s_spec, s_spec), (y_shape, s_shape, s_shape)
    scan = pltpu.VMEM((2, N_SLAB, SCAN_ROWS, LANES), F32)
    scratch = [
        pltpu.VMEM((MIX_TT, D_MODEL), BF16),
        pltpu.VMEM((MIX_TT, HP_W), F32),
        pltpu.VMEM((N_SLAB, SCAN_ROWS, LANES), F32),
        scan, scan, scan,
    ] + ([scan] if chunked else [])
    return pl.pallas_call(
        functools.partial(_rg_kernel, chunked=chunked),
        grid=(t // MIX_TT, N_HP),
        in_specs=in_specs,
        out_specs=out_specs,
        out_shape=out_shape,
        scratch_shapes=scratch,
        compiler_params=pltpu.CompilerParams(
            dimension_semantics=("parallel", "arbitrary"), vmem_limit_bytes=VMEM_LIMIT),
        name="rg_scan",
    )(x, mod, norm_g, h0f, h0b, w_in, w_in, conv_w, conv_b, w_gate, b_a, b_x, lam)


def _block_diag_pairs(w):
    n = w.shape[0]
    w = w.reshape(n, 2, N_HP, 2, RNN_HEAD_DIM, RNN_HEAD_DIM)
    z = jnp.zeros_like(w[:, :, :, 0])
    top = jnp.concatenate([w[:, :, :, 0], z], axis=-1)
    bot = jnp.concatenate([z, w[:, :, :, 1]], axis=-1)
    return jnp.concatenate([top, bot], axis=-2)


def _pool_layout(on_grid):
    run, n_run, pad_runs = (GRID_W, MIX_TT // GRID_W, POOL_HALO) if on_grid else (SEG, N_SEG, 0)
    pitch = run + POOL_HALO
    rows = 2 * POOL_EDGE + POOL_HALO + (n_run + 2 * pad_runs) * pitch
    return run, n_run, pad_runs, pitch, rows


def _window_count(half, pos, length):
    return (jnp.minimum(pos + half, length) - jnp.maximum(pos - half, 0)).astype(F32)


def _pool_kernel(x_ref, mod_ref, ng_ref, pw_ref, ps_ref, o_ref, inv_ref, d_ref, xb_ref, b0_ref, b1_ref,
                 *, on_grid):
    run, n_run, pad_runs, pitch, n_rows = _pool_layout(on_grid)
    base = lambda r: POOL_EDGE + POOL_HALO + (r + pad_runs) * pitch
    g = pl.program_id(1)

    @pl.when(g == 0)
    def _():
        x = x_ref[...]
        inv_ref[...] = lax.rsqrt(jnp.mean(x * x, axis=-1, keepdims=True) + EPS)

    row_id = lax.broadcasted_iota(jnp.int32, (MIX_TT, 1), 0)
    zero_edge = jnp.zeros((POOL_EDGE, LANES), F32)

    def window_sums(s, levels):
        src, bufs = xb_ref, [b0_ref, b1_ref]
        lo, hi = POOL_EDGE, n_rows - POOL_EDGE
        for lvl in range(levels):
            dst = bufs[lvl % 2]
            dn, up = (1, 0) if lvl == 0 else (2 ** (lvl - 1),) * 2
            dst[s, lo:hi, :] = src[s, lo - dn:hi - dn, :] + src[s, lo + up:hi + up, :]
            src = dst
        if on_grid:
            reach = 0
            for lvl in range(levels):
                dst = bufs[(levels + lvl) % 2]
                dn, up = (1, 0) if lvl == 0 else (2 ** (lvl - 1),) * 2
                reach += dn
                lo, hi = base(reach - pad_runs), base(n_run + pad_runs - reach)
                dst[s, lo:hi, :] = (src[s, lo - dn * pitch:hi - dn * pitch, :]
                                    + src[s, lo + up * pitch:hi + up * pitch, :])
                src = dst
        return src

    for gi, w in enumerate(POOL_WINDOWS):
        @pl.when(g == gi)
        def _(gi=gi, w=w):
            cols = slice(gi * POOL_GROUP_DIM, (gi + 1) * POOL_GROUP_DIM)
            scale = ng_ref[2:3, cols] * (1.0 + mod_ref[4:5, cols])
            xg = (x_ref[:, cols] * inv_ref[...]) * scale + mod_ref[3:4, cols]
            half = w // 2
            if on_grid:
                cnt = (_window_count(half, row_id % GRID_W, GRID_W)
                       * _window_count(half, row_id // GRID_W, n_run))
            else:
                cnt = _window_count(half, row_id % SEG, SEG)
            inv_cnt = 1.0 / cnt
            for s in range(POOL_GROUP_DIM // LANES):
                lanes = slice(s * LANES, (s + 1) * LANES)
                xb_ref[s] = jnp.zeros((n_rows, LANES), F32)
                for r in range(n_run):
                    xb_ref[s, base(r):base(r) + run, :] = xg[r * run:(r + 1) * run, lanes]
                for buf in (b0_ref, b1_ref):
                    buf[s, 0:POOL_EDGE, :] = zero_edge
                    buf[s, n_rows - POOL_EDGE:n_rows, :] = zero_edge
                tot = window_sums(s, w.bit_length() - 1)
                for r in range(n_run):
                    rows = slice(r * run, (r + 1) * run)
                    mean = tot[s, base(r):base(r) + run, :] * inv_cnt[rows, :]
                    d_ref[rows, lanes] = (mean - xb_ref[s, base(r):base(r) + run, :]).astype(BF16)
            o_ref[:, cols] = _dot(d_ref[...], pw_ref[...].astype(BF16)) * ps_ref[...]

    @pl.when(g == POOL_GROUPS - 1)
    def _():
        gain = mod_ref[5:6, :] * ng_ref[3:4, :]
        for r in range(MIX_TT // SEG):
            rows = slice(r * SEG, (r + 1) * SEG)
            y = o_ref[rows, :]
            inv = lax.rsqrt(jnp.mean(y * y, axis=-1, keepdims=True) + EPS)
            o_ref[rows, :] = x_ref[rows, :] + (y * inv) * gain


def _pool_mix(x, mod, norm_g, pool_w, pool_scale, *, layer, j, on_grid, row0):
    t = x.shape[0]
    row = _mod_row_map(1 if on_grid else None, row0)
    n_rows = _pool_layout(on_grid)[-1]
    buf = pltpu.VMEM((POOL_GROUP_DIM // LANES, n_rows, LANES), F32)
    return pl.pallas_call(
        functools.partial(_pool_kernel, on_grid=on_grid),
        grid=(t // MIX_TT, POOL_GROUPS),
        in_specs=[
            pl.BlockSpec((MIX_TT, D_MODEL), lambda i, g: (i, 0)),
            pl.BlockSpec((None, None, N_MOD, D_MODEL), lambda i, g: (layer, row(i), 0, 0)),
            pl.BlockSpec((None, 6, D_MODEL), lambda i, g: (layer, 0, 0)),
            pl.BlockSpec((None, None, POOL_GROUP_DIM, POOL_GROUP_DIM), lambda i, g: (j, g, 0, 0)),
            pl.BlockSpec((None, 1, POOL_GROUP_DIM), lambda i, g: (j, 0, g)),
        ],
        out_specs=pl.BlockSpec((MIX_TT, D_MODEL), lambda i, g: (i, 0)),
        out_shape=jax.ShapeDtypeStruct((t, D_MODEL), F32),
        scratch_shapes=[
            pltpu.VMEM((MIX_TT, 1), F32),
            pltpu.VMEM((MIX_TT, POOL_GROUP_DIM), BF16),
            buf, buf, buf,
        ],
        compiler_params=pltpu.CompilerParams(
            dimension_semantics=("parallel", "arbitrary"), vmem_limit_bytes=VMEM_LIMIT),
        name="pool_mix",
    )(x, mod, norm_g, pool_w, pool_scale)


def kernel(x_prompt, x_sample, state_rglru, c, c_ctx, mod_w, mod_b, norm_g, ffn_w1, ffn_w3, ffn_w2,
           rg_w_in, rg_conv_w, rg_conv_b, rg_w_a, rg_b_a, rg_w_x, rg_b_x, rg_lam, rg_w_out,
           pool_w, pool_scale):
    batch, seq, _ = x_prompt.shape
    dec_batch, dec_seq, _ = x_sample.shape
    assert seq == SEG and dec_seq == MIX_TT and dec_seq % GRID_W == 0
    assert 1 + dec_batch <= MOD_ROWS
    n_rg = rg_w_in.shape[0]

    cond = jnp.zeros((MOD_ROWS, D_MODEL), F32).at[0].set(c_ctx).at[1:1 + dec_batch].set(c)
    mod = _modulation(cond, mod_w, mod_b).reshape(DEPTH, MOD_ROWS, N_MOD, D_MODEL)

    wa, wx = _block_diag_pairs(rg_w_a), _block_diag_pairs(rg_w_x)
    w_gate = (0.5 * jnp.concatenate([wa[:, 0], wx[:, 0], wa[:, 1], wx[:, 1]], axis=-1)).astype(BF16)
    rg = (rg_w_in, rg_conv_w, rg_conv_b.reshape(n_rg, 1, D_RNN), w_gate, rg_b_a, rg_b_x, rg_lam)
    pool_scale3 = pool_scale.reshape(pool_scale.shape[0], 1, D_MODEL)
    zeros = jnp.zeros((batch, D_RNN), F32)
    ffn_w1, ffn_w3, ffn_w2 = (w.astype(BF16) for w in (ffn_w1, ffn_w3, ffn_w2))

    xc = x_prompt.reshape(batch * seq, D_MODEL)
    xs = x_sample.reshape(dec_batch * dec_seq, D_MODEL)
    groups = (
        dict(tiles_per_row=None, row0=0),
        dict(tiles_per_row=dec_seq // FFN_TM, row0=1),
    )
    ctx_states = []
    for l in range(DEPTH):
        j = l // 2
        xc = _ffn(xc, mod, norm_g, ffn_w1, ffn_w3, ffn_w2, layer=l, stage=0, **groups[0])
        xs = _ffn(xs, mod, norm_g, ffn_w1, ffn_w3, ffn_w2, layer=l, stage=0, **groups[1])
        if l % 2 == 0:
            yc, sf, sb = _rg_scan(xc, mod, norm_g, zeros, zeros, rg,
                                  layer=l, j=j, seq_len=seq, row0=0)
            ctx_states.append(jnp.stack([sf, sb], axis=1))
            ys = _rg_scan(xs, mod, norm_g, state_rglru[:, j, 0], state_rglru[:, j, 1], rg,
                          layer=l, j=j, seq_len=dec_seq, row0=1)
            xc = _out_proj(xc, yc, mod, norm_g, rg_w_out, layer=l, j=j, **groups[0])
            xs = _out_proj(xs, ys, mod, norm_g, rg_w_out, layer=l, j=j, **groups[1])
        else:
            xc = _pool_mix(xc, mod, norm_g, pool_w, pool_scale3, layer=l, j=j, on_grid=False, row0=0)
            xs = _pool_mix(xs, mod, norm_g, pool_w, pool_scale3, layer=l, j=j, on_grid=True, row0=1)
        xc = _ffn(xc, mod, norm_g, ffn_w1, ffn_w3, ffn_w2, layer=l, stage=2, **groups[0])
        xs = _ffn(xs, mod, norm_g, ffn_w1, ffn_w3, ffn_w2, layer=l, stage=2, **groups[1])
    new_state = jnp.stack(ctx_states, axis=1)
    return (xc.reshape(batch, seq, D_MODEL), xs.reshape(dec_batch, dec_seq, D_MODEL), new_state)
```

```python
import functools

import jax
import jax.numpy as jnp
from jax import lax
from jax.experimental import pallas as pl
from jax.experimental.pallas import tpu as pltpu

D_MODEL = 1024
DEPTH = 4
D_RNN = 1024
RNN_HEADS = 8
RNN_HEAD_DIM = D_RNN // RNN_HEADS
CONV_W = 4
RG_C = 8.0
GRID_W = 64
POOL_WINDOWS = (2, 4, 8, 16)
POOL_GROUPS = 4
POOL_GROUP_DIM = D_MODEL // POOL_GROUPS
D_FF = 2816
N_MOD = 9
EPS = 1e-6

F32 = jnp.float32
BF16 = jnp.bfloat16

MOD_ROWS = 8
FFN_TM = 1024
FFN_TF = 256
FFN_TN = 256
FFN_RB = 256
FFN_NA = D_FF // FFN_TF
FFN_NB = D_MODEL // FFN_TN
RES_TM = 1024
SEG = 256
N_SEG = 8
MIX_TT = SEG * N_SEG
SEG_GAP = 4
SEG_PITCH = SEG + SEG_GAP
SCAN_ROWS = SEG_GAP + N_SEG * SEG_PITCH + 4
RG_CHUNK = 2 * SEG
HP_W = 2 * RNN_HEAD_DIM
N_HP = D_RNN // HP_W
LANES = 128
N_SLAB = HP_W // LANES
POOL_HALO = max(POOL_WINDOWS) // 2
POOL_EDGE = 8
VMEM_LIMIT = 60 * 1024 * 1024


def _dot(a, b):
    return jnp.dot(a, b, preferred_element_type=F32)


def _rms(x, g):
    ms = jnp.mean(x * x, axis=-1, keepdims=True)
    return x * lax.rsqrt(ms + EPS) * g


def _sigmoid(x):
    return 0.5 * jnp.tanh(0.5 * x) + 0.5


def _norm_modulate(x, ng_ref, mod_ref, stage):
    g = ng_ref[2 * stage:2 * stage + 1, :]
    shift = mod_ref[3 * stage:3 * stage + 1, :]
    scale = mod_ref[3 * stage + 1:3 * stage + 2, :]
    inv = lax.rsqrt(jnp.mean(x * x, axis=-1, keepdims=True) + EPS)
    return (x * inv) * (g * (1.0 + scale)) + shift


def _mod_row_map(tile_rows, ctx_rows, lat_rows):
    def row(tile):
        start = tile * tile_rows
        return jnp.where(start < ctx_rows, 0, 1 + (start - ctx_rows) // lat_rows)
    return row


def _mod_kernel(c_ref, w_ref, b_ref, o_ref):
    c = c_ref[...]
    s = c * _sigmoid(c)
    o_ref[...] = _dot(s.astype(BF16), w_ref[...].astype(BF16)) + b_ref[...]


def _modulation(cond, mod_w, mod_b):
    tn = D_MODEL
    n_col = (N_MOD * D_MODEL) // tn
    return pl.pallas_call(
        _mod_kernel,
        grid=(DEPTH, n_col),
        in_specs=[
            pl.BlockSpec((MOD_ROWS, D_MODEL), lambda l, n: (0, 0)),
            pl.BlockSpec((None, D_MODEL, tn), lambda l, n: (l, 0, n)),
            pl.BlockSpec((None, 1, tn), lambda l, n: (l, 0, n)),
        ],
        out_specs=pl.BlockSpec((None, MOD_ROWS, tn), lambda l, n: (l, 0, n)),
        out_shape=jax.ShapeDtypeStruct((DEPTH, MOD_ROWS, N_MOD * D_MODEL), F32),
        compiler_params=pltpu.CompilerParams(
            dimension_semantics=("arbitrary", "arbitrary"), vmem_limit_bytes=VMEM_LIMIT),
        name="modulation",
    )(cond, mod_w, mod_b.reshape(DEPTH, 1, N_MOD * D_MODEL))


def _ffn_kernel(x_ref, xn_ref, mod_ref, modn_ref, ng_ref, w1_ref, w3_ref, w2_ref, o_ref, hn_ref, p_ref,
                *, stage):
    i = pl.program_id(0)
    slot = i % 2

    @pl.when(i == 0)
    def _():
        hn_ref[0] = _norm_modulate(x_ref[...], ng_ref, mod_ref, stage).astype(BF16)

    gain = ng_ref[2 * stage + 1:2 * stage + 2, :] * (0.5 * mod_ref[3 * stage + 2:3 * stage + 3, :])
    for r in range(FFN_TM // FFN_RB):
        rows = slice(r * FFN_RB, (r + 1) * FFN_RB)
        hn = hn_ref[slot, rows, :]
        for k in range(FFN_NA):
            cols = slice(k * FFN_TF, (k + 1) * FFN_TF)
            a = _dot(hn, w1_ref[:, cols])
            b = _dot(hn, w3_ref[:, cols])
            p_ref[k, rows, :] = (a * _sigmoid(a) * b).astype(BF16)
        accs = []
        for n in range(FFN_NB):
            cols = slice(n * FFN_TN, (n + 1) * FFN_TN)
            acc = _dot(p_ref[0, rows, :], w2_ref[0:FFN_TF, cols])
            for k in range(1, FFN_NA):
                acc += _dot(p_ref[k, rows, :], w2_ref[k * FFN_TF:(k + 1) * FFN_TF, cols])
            accs.append(acc)
        ss = sum(jnp.sum(acc * acc, axis=-1, keepdims=True) for acc in accs)
        inv = lax.rsqrt(ss * (1.0 / D_MODEL) + EPS)
        for n, acc in enumerate(accs):
            cols = slice(n * FFN_TN, (n + 1) * FFN_TN)
            o_ref[rows, cols] = x_ref[rows, cols] + (acc * inv) * gain[:, cols]

    hn_ref[1 - slot] = _norm_modulate(xn_ref[...], ng_ref, modn_ref, stage).astype(BF16)


def _ffn(x, mod, norm_g, w1, w3, w2, *, layer, stage, row, tile0, n_tiles):
    sub = 0 if stage == 0 else 1
    cur = lambda i: tile0 + i
    nxt = lambda i: tile0 + jnp.minimum(i + 1, n_tiles - 1)
    resident = dict(pipeline_mode=pl.Buffered(1))
    return pl.pallas_call(
        functools.partial(_ffn_kernel, stage=stage),
        grid=(n_tiles,),
        in_specs=[
            pl.BlockSpec((FFN_TM, D_MODEL), lambda i: (cur(i), 0)),
            pl.BlockSpec((FFN_TM, D_MODEL), lambda i: (nxt(i), 0)),
            pl.BlockSpec((None, None, N_MOD, D_MODEL), lambda i: (layer, row(cur(i)), 0, 0)),
            pl.BlockSpec((None, None, N_MOD, D_MODEL), lambda i: (layer, row(nxt(i)), 0, 0)),
            pl.BlockSpec((None, 6, D_MODEL), lambda i: (layer, 0, 0)),
            pl.BlockSpec((None, None, D_MODEL, D_FF), lambda i: (layer, sub, 0, 0), **resident),
            pl.BlockSpec((None, None, D_MODEL, D_FF), lambda i: (layer, sub, 0, 0), **resident),
            pl.BlockSpec((None, None, D_FF, D_MODEL), lambda i: (layer, sub, 0, 0), **resident),
        ],
        out_specs=pl.BlockSpec((FFN_TM, D_MODEL), lambda i: (i, 0)),
        out_shape=jax.ShapeDtypeStruct((n_tiles * FFN_TM, D_MODEL), F32),
        scratch_shapes=[
            pltpu.VMEM((2, FFN_TM, D_MODEL), BF16),
            pltpu.VMEM((FFN_NA, FFN_TM, FFN_TF), BF16),
        ],
        compiler_params=pltpu.CompilerParams(
            dimension_semantics=("arbitrary",), vmem_limit_bytes=VMEM_LIMIT),
        name="ffn",
    )(x, x, mod, mod, norm_g, w1, w3, w2)


def _out_proj_kernel(x_ref, yc_ref, yl_ref, mod_ref, ng_ref, w_ref, o_ref, *, n_ctx):
    y_in = jnp.where(pl.program_id(0) < n_ctx, yc_ref[...], yl_ref[...])
    y = _dot(y_in, w_ref[...].astype(BF16))
    o_ref[...] = x_ref[...] + mod_ref[5:6, :] * _rms(y, ng_ref[3:4, :])


def _out_proj(x, y_ctx, y_lat, mod, norm_g, w_out, *, layer, j, row):
    n_ctx = y_ctx.shape[0] // RES_TM
    n_lat = y_lat.shape[0] // RES_TM
    return pl.pallas_call(
        functools.partial(_out_proj_kernel, n_ctx=n_ctx),
        grid=(n_ctx + n_lat,),
        in_specs=[
            pl.BlockSpec((RES_TM, D_MODEL), lambda i: (i, 0)),
            pl.BlockSpec((RES_TM, D_RNN), lambda i: (jnp.minimum(i, n_ctx - 1), 0)),
            pl.BlockSpec((RES_TM, D_RNN), lambda i: (jnp.maximum(i - n_ctx, 0), 0)),
            pl.BlockSpec((None, None, N_MOD, D_MODEL), lambda i: (layer, row(i), 0, 0)),
            pl.BlockSpec((None, 6, D_MODEL), lambda i: (layer, 0, 0)),
            pl.BlockSpec((None, D_RNN, D_MODEL), lambda i: (j, 0, 0)),
        ],
        out_specs=pl.BlockSpec((RES_TM, D_MODEL), lambda i: (i, 0)),
        out_shape=jax.ShapeDtypeStruct(x.shape, F32),
        compiler_params=pltpu.CompilerParams(
            dimension_semantics=("parallel",), vmem_limit_bytes=VMEM_LIMIT),
        name="out_proj",
    )(x, y_ctx, y_lat, mod, norm_g, w_out)


def _seg_base(k):
    return SEG_GAP + k * SEG_PITCH


def _seg_rows(k, offset=0):
    return pl.ds(_seg_base(k) + offset, SEG)


def _rg_kernel(x_ref, mod_ref, ng_ref, h0f_ref, h0b_ref, wig_ref, wix_ref, cw_ref, cb_ref, wg_ref,
               ba_ref, bx_ref, lam_ref, *rest, chunked):
    if chunked:
        y_ref, hn_ref, gg_ref, xr_ref, a_ref, u_ref, h_ref, p_ref = rest
    else:
        y_ref, sf_ref, sb_ref, hn_ref, gg_ref, xr_ref, a_ref, u_ref, h_ref = rest
    hp = pl.program_id(1)

    @pl.when(hp == 0)
    def _():
        hn_ref[...] = _norm_modulate(x_ref[...], ng_ref, mod_ref, 1).astype(BF16)

    wig = wig_ref[...].astype(BF16)
    wix = wix_ref[...].astype(BF16)
    seg_per_chunk = RG_CHUNK // SEG
    zero2 = jnp.zeros((2, LANES), F32)

    def project(c):
        rows = slice(c * RG_CHUNK, (c + 1) * RG_CHUNK)
        hn = hn_ref[rows, :]
        gg_ref[rows, :] = jax.nn.gelu(_dot(hn, wig))
        xr = _dot(hn, wix)
        for s in range(N_SLAB):
            lanes = slice(s * LANES, (s + 1) * LANES)
            for kk in range(seg_per_chunk):
                k = c * seg_per_chunk + kk
                base = _seg_base(k)
                seg = xr[kk * SEG:(kk + 1) * SEG, lanes]
                xr_ref[s, _seg_rows(k), :] = seg
                inside = chunked and k > 0
                xr_ref[s, base - SEG_GAP:base - 2, :] = seg[0:2, :] if inside else zero2
                inside = chunked and k < N_SEG - 1
                xr_ref[s, base + SEG + 2:base + SEG + SEG_GAP, :] = seg[SEG - 2:SEG, :] if inside else zero2
                if k == 0:
                    xr_ref[s, base - 2:base, :] = zero2
                if k == N_SEG - 1:
                    xr_ref[s, base + SEG:base + SEG + 2, :] = zero2

    z = -lam_ref[...]
    softplus = jnp.maximum(z, 0.0) + jnp.log1p(jnp.exp(-jnp.abs(z)))
    half_rate = (0.5 * (-RG_C)) * softplus
    half_ba = 0.5 * ba_ref[...]
    half_bx = 0.5 * bx_ref[...]
    cw = cw_ref[...]
    cb = cb_ref[...]

    def gates(k):
        taps = []
        for s in range(N_SLAB):
            lanes = slice(s * LANES, (s + 1) * LANES)
            acc = cb[:, lanes] + cw[0:1, lanes] * xr_ref[s, _seg_rows(k, -1), :]
            for j in range(1, CONV_W):
                acc += cw[j:j + 1, lanes] * xr_ref[s, _seg_rows(k, j - 1), :]
            taps.append(acc)
        xk = jnp.concatenate(taps, axis=1)
        pre = _dot(xk.astype(BF16), wg_ref[...])
        xh = 0.5 * xk
        for d in range(2):
            tr = jnp.tanh(pre[:, (2 * d) * HP_W:(2 * d + 1) * HP_W] + half_ba[d:d + 1, :])
            ti = jnp.tanh(pre[:, (2 * d + 1) * HP_W:(2 * d + 2) * HP_W] + half_bx[d:d + 1, :])
            a = jnp.exp(tr * half_rate[d:d + 1, :] + half_rate[d:d + 1, :])
            v = 1.0 - a * a
            beta = jnp.where(v > 0.0, v * lax.rsqrt(v), 0.0)
            u = beta * (xh * ti + xh)
            for s in range(N_SLAB):
                lanes = slice(s * LANES, (s + 1) * LANES)
                a_ref[d, s, _seg_rows(k), :] = a[:, lanes]
                u_ref[d, s, _seg_rows(k), :] = u[:, lanes]

    n_chunk = MIX_TT // RG_CHUNK
    project(0)
    for c in range(n_chunk):
        if c + 1 < n_chunk:
            project(c + 1)
        first = c * seg_per_chunk - 1
        for k in range(max(first, 0), first + seg_per_chunk):
            gates(k)
    gates(N_SEG - 1)

    def step(t, carry):
        hs, ps = carry
        new_h, new_p = [], []
        for d in range(2):
            row = SEG_GAP + (t if d == 0 else SEG - 1 - t)
            idx = pl.ds(row, N_SEG, stride=SEG_PITCH)
            for s in range(N_SLAB):
                a = a_ref[d, s, idx, :]
                h = a * hs[d * N_SLAB + s] + u_ref[d, s, idx, :]
                h_ref[d, s, idx, :] = h
                new_h.append(h)
                if chunked:
                    p = a * ps[d * N_SLAB + s]
                    p_ref[d, s, idx, :] = p
                    new_p.append(p)
        return tuple(new_h), tuple(new_p)

    if chunked:
        h_init = tuple(jnp.zeros((N_SEG, LANES), F32) for _ in range(2 * N_SLAB))
        p_init = tuple(jnp.ones((N_SEG, LANES), F32) for _ in range(2 * N_SLAB))
    else:
        h0 = (h0f_ref[...], h0b_ref[...])
        h_init = tuple(h0[d][:, s * LANES:(s + 1) * LANES] for d in range(2) for s in range(N_SLAB))
        p_init = ()
    h_fin, _ = lax.fori_loop(0, SEG, step, (h_init, p_init), unroll=8)

    if chunked:
        b = pl.program_id(0)
        h0 = (h0f_ref[pl.ds(b, 1), :], h0b_ref[pl.ds(b, 1), :])
        for s in range(N_SLAB):
            lanes = slice(s * LANES, (s + 1) * LANES)
            carry = h0[0][:, lanes]
            for k in range(N_SEG):
                h = h_ref[0, s, _seg_rows(k), :] + p_ref[0, s, _seg_rows(k), :] * carry
                h_ref[0, s, _seg_rows(k), :] = h
                carry = h[SEG - 1:SEG, :]
            carry = h0[1][:, lanes]
            for k in reversed(range(N_SEG)):
                hb = h_ref[1, s, _seg_rows(k), :] + p_ref[1, s, _seg_rows(k), :] * carry
                carry = hb[0:1, :]
                rows = slice(k * SEG, (k + 1) * SEG)
                y = (h_ref[0, s, _seg_rows(k), :] + hb) * gg_ref[rows, lanes]
                y_ref[rows, lanes] = y.astype(BF16)
    else:
        for s in range(N_SLAB):
            lanes = slice(s * LANES, (s + 1) * LANES)
            sf_ref[:, lanes] = h_fin[s]
            sb_ref[:, lanes] = h_fin[N_SLAB + s]
            for k in range(N_SEG):
                rows = slice(k * SEG, (k + 1) * SEG)
                y = (h_ref[0, s, _seg_rows(k), :] + h_ref[1, s, _seg_rows(k), :]) * gg_ref[rows, lanes]
                y_ref[rows, lanes] = y.astype(BF16)


def _rg_scan(x, mod, norm_g, h0f, h0b, rg, *, layer, j, chunked, row, tile0, n_tiles):
    t = n_tiles * MIX_TT
    w_in, conv_w, conv_b, w_gate, b_a, b_x, lam = rg
    n_state = h0f.shape[0]
    state_block = (n_state, HP_W) if chunked else (N_SEG, HP_W)
    state_map = (lambda i, hp: (0, hp)) if chunked else (lambda i, hp: (i, hp))
    in_specs = [
        pl.BlockSpec((MIX_TT, D_MODEL), lambda i, hp: (tile0 + i, 0)),
        pl.BlockSpec((None, None, N_MOD, D_MODEL), lambda i, hp: (layer, row(tile0 + i), 0, 0)),
        pl.BlockSpec((None, 6, D_MODEL), lambda i, hp: (layer, 0, 0)),
        pl.BlockSpec(state_block, state_map),
        pl.BlockSpec(state_block, state_map),
        pl.BlockSpec((None, D_MODEL, HP_W), lambda i, hp: (j, 0, hp)),
        pl.BlockSpec((None, D_MODEL, HP_W), lambda i, hp: (j, 0, N_HP + hp)),
        pl.BlockSpec((None, CONV_W, HP_W), lambda i, hp: (j, 0, hp)),
        pl.BlockSpec((None, 1, HP_W), lambda i, hp: (j, 0, hp)),
        pl.BlockSpec((None, None, HP_W, 4 * HP_W), lambda i, hp: (j, hp, 0, 0)),
        pl.BlockSpec((None, 2, HP_W), lambda i, hp: (j, 0, hp)),
        pl.BlockSpec((None, 2, HP_W), lambda i, hp: (j, 0, hp)),
        pl.BlockSpec((None, 2, HP_W), lambda i, hp: (j, 0, hp)),
    ]
    y_spec = pl.BlockSpec((MIX_TT, HP_W), lambda i, hp: (i, hp))
    y_shape = jax.ShapeDtypeStruct((t, D_RNN), BF16)
    if chunked:
        out_specs, out_shape = y_spec, y_shape
    else:
        s_spec = pl.BlockSpec((N_SEG, HP_W), lambda i, hp: (i, hp))
        s_shape = jax.ShapeDtypeStruct((t // SEG, D_RNN), F32)
        out_specs, out_shape = (y_spec, s_spec, s_spec), (y_shape, s_shape, s_shape)
    scan = pltpu.VMEM((2, N_SLAB, SCAN_ROWS, LANES), F32)
    scratch = [
        pltpu.VMEM((MIX_TT, D_MODEL), BF16),
        pltpu.VMEM((MIX_TT, HP_W), F32),
        pltpu.VMEM((N_SLAB, SCAN_ROWS, LANES), F32),
        scan, scan, scan,
    ] + ([scan] if chunked else [])
    return pl.pallas_call(
        functools.partial(_rg_kernel, chunked=chunked),
        grid=(t // MIX_TT, N_HP),
        in_specs=in_specs,
        out_specs=out_specs,
        out_shape=out_shape,
        scratch_shapes=scratch,
        compiler_params=pltpu.CompilerParams(
            dimension_semantics=("parallel", "arbitrary"), vmem_limit_bytes=VMEM_LIMIT),
        name="rg_scan",
    )(x, mod, norm_g, h0f, h0b, w_in, w_in, conv_w, conv_b, w_gate, b_a, b_x, lam)


def _block_diag_pairs(w):
    n = w.shape[0]
    w = w.reshape(n, 2, N_HP, 2, RNN_HEAD_DIM, RNN_HEAD_DIM)
    z = jnp.zeros_like(w[:, :, :, 0])
    top = jnp.concatenate([w[:, :, :, 0], z], axis=-1)
    bot = jnp.concatenate([z, w[:, :, :, 1]], axis=-1)
    return jnp.concatenate([top, bot], axis=-2)


def _pool_layout(on_grid):
    run, n_run, pad_runs = (GRID_W, MIX_TT // GRID_W, POOL_HALO) if on_grid else (SEG, N_SEG, 0)
    pitch = run + POOL_HALO
    rows = 2 * POOL_EDGE + POOL_HALO + (n_run + 2 * pad_runs) * pitch
    return run, n_run, pad_runs, pitch, rows


def _window_count(half, pos, length):
    return (jnp.minimum(pos + half, length) - jnp.maximum(pos - half, 0)).astype(F32)


def _pool_kernel(x_ref, mod_ref, ng_ref, pw_ref, ps_ref, o_ref, inv_ref, d_ref, xb_ref, b0_ref, b1_ref,
                 *, n_ctx):
    i = pl.program_id(0)
    g = pl.program_id(1)

    @pl.when(g == 0)
    def _():
        x = x_ref[...]
        inv_ref[...] = lax.rsqrt(jnp.mean(x * x, axis=-1, keepdims=True) + EPS)

    row_id = lax.broadcasted_iota(jnp.int32, (MIX_TT, 1), 0)
    zero_edge = jnp.zeros((POOL_EDGE, LANES), F32)

    def pool_group(gi, w, on_grid):
        run, n_run, pad_runs, pitch, n_rows = _pool_layout(on_grid)
        base = lambda r: POOL_EDGE + POOL_HALO + (r + pad_runs) * pitch
        levels = w.bit_length() - 1
        half = w // 2

        def window_sums(s):
            src, bufs = xb_ref, [b0_ref, b1_ref]
            lo, hi = POOL_EDGE, n_rows - POOL_EDGE
            for lvl in range(levels):
                dst = bufs[lvl % 2]
                dn, up = (1, 0) if lvl == 0 else (2 ** (lvl - 1),) * 2
                dst[s, lo:hi, :] = src[s, lo - dn:hi - dn, :] + src[s, lo + up:hi + up, :]
                src = dst
            if on_grid:
                reach = 0
                for lvl in range(levels):
                    dst = bufs[(levels + lvl) % 2]
                    dn, up = (1, 0) if lvl == 0 else (2 ** (lvl - 1),) * 2
                    reach += dn
                    lo, hi = base(reach - pad_runs), base(n_run + pad_runs - reach)
                    dst[s, lo:hi, :] = (src[s, lo - dn * pitch:hi - dn * pitch, :]
                                        + src[s, lo + up * pitch:hi + up * pitch, :])
                    src = dst
            return src

        cols = slice(gi * POOL_GROUP_DIM, (gi + 1) * POOL_GROUP_DIM)
        scale = ng_ref[2:3, cols] * (1.0 + mod_ref[4:5, cols])
        xg = (x_ref[:, cols] * inv_ref[...]) * scale + mod_ref[3:4, cols]
        if on_grid:
            cnt = (_window_count(half, row_id % GRID_W, GRID_W)
                   * _window_count(half, row_id // GRID_W, n_run))
        else:
            cnt = _window_count(half, row_id % SEG, SEG)
        inv_cnt = 1.0 / cnt
        for s in range(POOL_GROUP_DIM // LANES):
            lanes = slice(s * LANES, (s + 1) * LANES)
            xb_ref[s, 0:n_rows, :] = jnp.zeros((n_rows, LANES), F32)
            for r in range(n_run):
                xb_ref[s, base(r):base(r) + run, :] = xg[r * run:(r + 1) * run, lanes]
            for buf in (b0_ref, b1_ref):
                buf[s, 0:POOL_EDGE, :] = zero_edge
                buf[s, n_rows - POOL_EDGE:n_rows, :] = zero_edge
            tot = window_sums(s)
            for r in range(n_run):
                rows = slice(r * run, (r + 1) * run)
                mean = tot[s, base(r):base(r) + run, :] * inv_cnt[rows, :]
                d_ref[rows, lanes] = (mean - xb_ref[s, base(r):base(r) + run, :]).astype(BF16)
        o_ref[:, cols] = _dot(d_ref[...], pw_ref[...].astype(BF16)) * ps_ref[...]

    for on_grid in (False, True):
        for gi, w in enumerate(POOL_WINDOWS):
            pl.when((g == gi) & ((i >= n_ctx) == on_grid))(functools.partial(pool_group, gi, w, on_grid))

    @pl.when(g == POOL_GROUPS - 1)
    def _():
        gain = mod_ref[5:6, :] * ng_ref[3:4, :]
        for r in range(MIX_TT // SEG):
            rows = slice(r * SEG, (r + 1) * SEG)
            y = o_ref[rows, :]
            inv = lax.rsqrt(jnp.mean(y * y, axis=-1, keepdims=True) + EPS)
            o_ref[rows, :] = x_ref[rows, :] + (y * inv) * gain


def _pool_mix(x, mod, norm_g, pool_w, pool_scale, *, layer, j, row, n_ctx):
    t = x.shape[0]
    n_rows = max(_pool_layout(False)[-1], _pool_layout(True)[-1])
    buf = pltpu.VMEM((POOL_GROUP_DIM // LANES, n_rows, LANES), F32)
    return pl.pallas_call(
        functools.partial(_pool_kernel, n_ctx=n_ctx),
        grid=(t // MIX_TT, POOL_GROUPS),
        in_specs=[
            pl.BlockSpec((MIX_TT, D_MODEL), lambda i, g: (i, 0)),
            pl.BlockSpec((None, None, N_MOD, D_MODEL), lambda i, g: (layer, row(i), 0, 0)),
            pl.BlockSpec((None, 6, D_MODEL), lambda i, g: (layer, 0, 0)),
            pl.BlockSpec((None, None, POOL_GROUP_DIM, POOL_GROUP_DIM), lambda i, g: (j, g, 0, 0)),
            pl.BlockSpec((None, 1, POOL_GROUP_DIM), lambda i, g: (j, 0, g)),
        ],
        out_specs=pl.BlockSpec((MIX_TT, D_MODEL), lambda i, g: (i, 0)),
        out_shape=jax.ShapeDtypeStruct((t, D_MODEL), F32),
        scratch_shapes=[
            pltpu.VMEM((MIX_TT, 1), F32),
            pltpu.VMEM((MIX_TT, POOL_GROUP_DIM), BF16),
            buf, buf, buf,
        ],
        compiler_params=pltpu.CompilerParams(
            dimension_semantics=("parallel", "arbitrary"), vmem_limit_bytes=VMEM_LIMIT),
        name="pool_mix",
    )(x, mod, norm_g, pool_w, pool_scale)


def kernel(x_prompt, x_sample, state_rglru, c, c_ctx, mod_w, mod_b, norm_g, ffn_w1, ffn_w3, ffn_w2,
           rg_w_in, rg_conv_w, rg_conv_b, rg_w_a, rg_b_a, rg_w_x, rg_b_x, rg_lam, rg_w_out,
           pool_w, pool_scale):
    batch, seq, _ = x_prompt.shape
    dec_batch, dec_seq, _ = x_sample.shape
    ctx_rows, lat_rows = batch * seq, dec_batch * dec_seq
    assert seq == SEG and dec_seq == MIX_TT and dec_seq % GRID_W == 0
    assert ctx_rows % MIX_TT == 0 and MIX_TT % FFN_TM == 0 and FFN_TM == RES_TM
    assert 1 + dec_batch <= MOD_ROWS
    n_rg = rg_w_in.shape[0]

    cond = jnp.zeros((MOD_ROWS, D_MODEL), F32).at[0].set(c_ctx).at[1:1 + dec_batch].set(c)
    mod = _modulation(cond, mod_w, mod_b).reshape(DEPTH, MOD_ROWS, N_MOD, D_MODEL)

    wa, wx = _block_diag_pairs(rg_w_a), _block_diag_pairs(rg_w_x)
    w_gate = (0.5 * jnp.concatenate([wa[:, 0], wx[:, 0], wa[:, 1], wx[:, 1]], axis=-1)).astype(BF16)
    rg = (rg_w_in, rg_conv_w, rg_conv_b.reshape(n_rg, 1, D_RNN), w_gate, rg_b_a, rg_b_x, rg_lam)
    pool_scale3 = pool_scale.reshape(pool_scale.shape[0], 1, D_MODEL)
    zeros = jnp.zeros((batch, D_RNN), F32)
    ffn_w1, ffn_w3, ffn_w2 = (w.astype(BF16) for w in (ffn_w1, ffn_w3, ffn_w2))

    x = jnp.concatenate([x_prompt.reshape(ctx_rows, D_MODEL), x_sample.reshape(lat_rows, D_MODEL)], axis=0)
    ffn_row = _mod_row_map(FFN_TM, ctx_rows, dec_seq)
    mix_row = _mod_row_map(MIX_TT, ctx_rows, dec_seq)
    n_ffn_ctx, n_ffn_lat = ctx_rows // FFN_TM, lat_rows // FFN_TM
    n_mix_ctx, n_mix_lat = ctx_rows // MIX_TT, lat_rows // MIX_TT
    ffn = functools.partial(_ffn, mod=mod, norm_g=norm_g, w1=ffn_w1, w3=ffn_w3, w2=ffn_w2, row=ffn_row)

    ctx_states = []
    for l in range(DEPTH):
        j = l // 2
        x = ffn(x, layer=l, stage=0, tile0=0, n_tiles=n_ffn_ctx + n_ffn_lat)
        if l % 2 == 0:
            scan = functools.partial(_rg_scan, x, mod, norm_g, rg=rg, layer=l, j=j, row=mix_row)
            y_ctx, sf, sb = scan(zeros, zeros, chunked=False, tile0=0, n_tiles=n_mix_ctx)
            y_lat = scan(state_rglru[:, j, 0], state_rglru[:, j, 1], chunked=True,
                         tile0=n_mix_ctx, n_tiles=n_mix_lat)
            ctx_states.append(jnp.stack([sf, sb], axis=1))
            x = _out_proj(x, y_ctx, y_lat, mod, norm_g, rg_w_out, layer=l, j=j, row=ffn_row)
        else:
            x = _pool_mix(x, mod, norm_g, pool_w, pool_scale3, layer=l, j=j, row=mix_row, n_ctx=n_mix_ctx)
        if l + 1 < DEPTH:
            x = ffn(x, layer=l, stage=2, tile0=0, n_tiles=n_ffn_ctx + n_ffn_lat)
    y_prompt = ffn(x, layer=DEPTH - 1, stage=2, tile0=0, n_tiles=n_ffn_ctx)
    y_sample = ffn(x, layer=DEPTH - 1, stage=2, tile0=n_ffn_ctx, n_tiles=n_ffn_lat)
    new_state = jnp.stack(ctx_states, axis=1)
    return (y_prompt.reshape(batch, seq, D_MODEL), y_sample.reshape(dec_batch, dec_seq, D_MODEL), new_state)
```

```python
import functools

import jax
import jax.numpy as jnp
from jax import lax
from jax.experimental import pallas as pl
from jax.experimental.pallas import tpu as pltpu

D_MODEL = 1024
DEPTH = 4
D_RNN = 1024
RNN_HEADS = 8
RNN_HEAD_DIM = D_RNN // RNN_HEADS
CONV_W = 4
RG_C = 8.0
GRID_W = 64
POOL_WINDOWS = (2, 4, 8, 16)
POOL_GROUPS = 4
POOL_GROUP_DIM = D_MODEL // POOL_GROUPS
D_FF = 2816
N_MOD = 9
EPS = 1e-6

F32 = jnp.float32
BF16 = jnp.bfloat16

MOD_ROWS = 8
MOD_TN = 1024
FFN_TM = 1024
FFN_TF = 256
FFN_TN = 256
FFN_RB = 256
FFN_NA = D_FF // FFN_TF
FFN_NB = D_MODEL // FFN_TN
RES_TM = 1024
SEG = 256
N_SEG = 8
MIX_TT = SEG * N_SEG
SEG_GAP = 4
SEG_PITCH = SEG + SEG_GAP
SCAN_ROWS = SEG_GAP + N_SEG * SEG_PITCH + 4
RG_CHUNK = 2 * SEG
HP_W = 2 * RNN_HEAD_DIM
N_HP = D_RNN // HP_W
LANES = 128
N_SLAB = HP_W // LANES
POOL_HALO = max(POOL_WINDOWS) // 2
POOL_EDGE = 8
VMEM_LIMIT = 60 * 1024 * 1024


def _dot(a, b):
    return jnp.dot(a, b, preferred_element_type=F32)


def _rms(x, g):
    ms = jnp.mean(x * x, axis=-1, keepdims=True)
    return x * lax.rsqrt(ms + EPS) * g


def _sigmoid(x):
    return 0.5 * jnp.tanh(0.5 * x) + 0.5


def _norm_modulate(x, ng_ref, mod_ref, stage):
    g = ng_ref[2 * stage:2 * stage + 1, :]
    shift = mod_ref[3 * stage:3 * stage + 1, :]
    scale = mod_ref[3 * stage + 1:3 * stage + 2, :]
    inv = lax.rsqrt(jnp.mean(x * x, axis=-1, keepdims=True) + EPS)
    return (x * inv) * (g * (1.0 + scale)) + shift


def _mod_row_map(tile_rows, ctx_rows, lat_rows):
    def row(tile):
        start = tile * tile_rows
        return jnp.where(start < ctx_rows, 0, 1 + (start - ctx_rows) // lat_rows)
    return row


def _mod_kernel(c_ref, w_ref, b_ref, o_ref):
    c = c_ref[...]
    s = c * _sigmoid(c)
    o_ref[...] = _dot(s.astype(BF16), w_ref[...].astype(BF16)) + b_ref[...]


def _modulation(cond, mod_w, mod_b):
    tn = MOD_TN
    n_col = (N_MOD * D_MODEL) // tn
    return pl.pallas_call(
        _mod_kernel,
        grid=(DEPTH, n_col),
        in_specs=[
            pl.BlockSpec((MOD_ROWS, D_MODEL), lambda l, n: (0, 0)),
            pl.BlockSpec((None, D_MODEL, tn), lambda l, n: (l, 0, n)),
            pl.BlockSpec((None, 1, tn), lambda l, n: (l, 0, n)),
        ],
        out_specs=pl.BlockSpec((None, MOD_ROWS, tn), lambda l, n: (l, 0, n)),
        out_shape=jax.ShapeDtypeStruct((DEPTH, MOD_ROWS, N_MOD * D_MODEL), F32),
        compiler_params=pltpu.CompilerParams(
            dimension_semantics=("arbitrary", "arbitrary"), vmem_limit_bytes=VMEM_LIMIT),
        name="modulation",
    )(cond, mod_w, mod_b.reshape(DEPTH, 1, N_MOD * D_MODEL))


def _ffn_kernel(x_ref, xn_ref, mod_ref, modn_ref, ng_ref, w1_ref, w3_ref, w2_ref, o_ref, hn_ref, p_ref,
                *, stage):
    @pl.when(pl.program_id(0) == 0)
    def _():
        hn_ref[...] = _norm_modulate(x_ref[...], ng_ref, mod_ref, stage).astype(BF16)

    gain = ng_ref[2 * stage + 1:2 * stage + 2, :] * (0.5 * mod_ref[3 * stage + 2:3 * stage + 3, :])
    n_blocks = FFN_TM // FFN_RB
    block_rows = [slice(r * FFN_RB, (r + 1) * FFN_RB) for r in range(n_blocks)]

    def up_project(rows):
        for k in range(FFN_NA):
            cols = slice(k * FFN_TF, (k + 1) * FFN_TF)
            a = _dot(hn_ref[rows, :], w1_ref[:, cols])
            b = _dot(hn_ref[rows, :], w3_ref[:, cols])
            p_ref[k, rows, :] = (a * _sigmoid(a) * b).astype(BF16)

    def down_project(r):
        rows = block_rows[r]
        piece = FFN_RB // FFN_NB
        accs = []
        for n in range(FFN_NB):
            nrows = slice(r * FFN_RB + n * piece, r * FFN_RB + (n + 1) * piece)
            hn_ref[nrows, :] = _norm_modulate(xn_ref[nrows, :], ng_ref, modn_ref, stage).astype(BF16)
            cols = slice(n * FFN_TN, (n + 1) * FFN_TN)
            acc = _dot(p_ref[0, rows, :], w2_ref[0:FFN_TF, cols])
            for k in range(1, FFN_NA):
                acc += _dot(p_ref[k, rows, :], w2_ref[k * FFN_TF:(k + 1) * FFN_TF, cols])
            accs.append(acc)
        ss = sum(jnp.sum(acc * acc, axis=-1, keepdims=True) for acc in accs)
        inv = lax.rsqrt(ss * (1.0 / D_MODEL) + EPS)
        for n, acc in enumerate(accs):
            cols = slice(n * FFN_TN, (n + 1) * FFN_TN)
            o_ref[rows, cols] = x_ref[rows, cols] + (acc * inv) * gain[:, cols]

    up_project(block_rows[0])
    for r in range(n_blocks):
        if r + 1 < n_blocks:
            up_project(block_rows[r + 1])
        down_project(r)


def _ffn(x, mod, norm_g, w1, w3, w2, *, layer, stage, row, tile0, n_tiles):
    sub = 0 if stage == 0 else 1
    cur = lambda i: tile0 + i
    nxt = lambda i: tile0 + jnp.minimum(i + 1, n_tiles - 1)
    resident = dict(pipeline_mode=pl.Buffered(1))
    return pl.pallas_call(
        functools.partial(_ffn_kernel, stage=stage),
        grid=(n_tiles,),
        in_specs=[
            pl.BlockSpec((FFN_TM, D_MODEL), lambda i: (cur(i), 0)),
            pl.BlockSpec((FFN_TM, D_MODEL), lambda i: (nxt(i), 0)),
            pl.BlockSpec((None, None, N_MOD, D_MODEL), lambda i: (layer, row(cur(i)), 0, 0)),
            pl.BlockSpec((None, None, N_MOD, D_MODEL), lambda i: (layer, row(nxt(i)), 0, 0)),
            pl.BlockSpec((None, 6, D_MODEL), lambda i: (layer, 0, 0)),
            pl.BlockSpec((None, None, D_MODEL, D_FF), lambda i: (layer, sub, 0, 0), **resident),
            pl.BlockSpec((None, None, D_MODEL, D_FF), lambda i: (layer, sub, 0, 0), **resident),
            pl.BlockSpec((None, None, D_FF, D_MODEL), lambda i: (layer, sub, 0, 0), **resident),
        ],
        out_specs=pl.BlockSpec((FFN_TM, D_MODEL), lambda i: (i, 0)),
        out_shape=jax.ShapeDtypeStruct((n_tiles * FFN_TM, D_MODEL), F32),
        scratch_shapes=[
            pltpu.VMEM((FFN_TM, D_MODEL), BF16),
            pltpu.VMEM((FFN_NA, FFN_TM, FFN_TF), BF16),
        ],
        compiler_params=pltpu.CompilerParams(
            dimension_semantics=("arbitrary",), vmem_limit_bytes=VMEM_LIMIT),
        name="ffn",
    )(x, x, mod, mod, norm_g, w1, w3, w2)


def _out_proj_kernel(x_ref, yc_ref, yl_ref, mod_ref, ng_ref, w_ref, o_ref, *, n_ctx):
    y_in = jnp.where(pl.program_id(0) < n_ctx, yc_ref[...], yl_ref[...])
    y = _dot(y_in, w_ref[...].astype(BF16))
    o_ref[...] = x_ref[...] + mod_ref[5:6, :] * _rms(y, ng_ref[3:4, :])


def _out_proj(x, y_ctx, y_lat, mod, norm_g, w_out, *, layer, j, row):
    n_ctx = y_ctx.shape[0] // RES_TM
    n_lat = y_lat.shape[0] // RES_TM
    return pl.pallas_call(
        functools.partial(_out_proj_kernel, n_ctx=n_ctx),
        grid=(n_ctx + n_lat,),
        in_specs=[
            pl.BlockSpec((RES_TM, D_MODEL), lambda i: (i, 0)),
            pl.BlockSpec((RES_TM, D_RNN), lambda i: (jnp.minimum(i, n_ctx - 1), 0)),
            pl.BlockSpec((RES_TM, D_RNN), lambda i: (jnp.maximum(i - n_ctx, 0), 0)),
            pl.BlockSpec((None, None, N_MOD, D_MODEL), lambda i: (layer, row(i), 0, 0)),
            pl.BlockSpec((None, 6, D_MODEL), lambda i: (layer, 0, 0)),
            pl.BlockSpec((None, D_RNN, D_MODEL), lambda i: (j, 0, 0)),
        ],
        out_specs=pl.BlockSpec((RES_TM, D_MODEL), lambda i: (i, 0)),
        out_shape=jax.ShapeDtypeStruct(x.shape, F32),
        compiler_params=pltpu.CompilerParams(
            dimension_semantics=("parallel",), vmem_limit_bytes=VMEM_LIMIT),
        name="out_proj",
    )(x, y_ctx, y_lat, mod, norm_g, w_out)


def _seg_base(k):
    return SEG_GAP + k * SEG_PITCH


def _seg_rows(k, offset=0):
    return pl.ds(_seg_base(k) + offset, SEG)


def _rg_kernel(x_ref, mod_ref, ng_ref, h0f_ref, h0b_ref, wig_ref, wix_ref, cw_ref, cb_ref, wg_ref,
               ba_ref, bx_ref, lam_ref, *rest, chunked):
    if chunked:
        y_ref, hn_ref, gg_ref, xr_ref, a_ref, u_ref, h_ref, p_ref = rest
    else:
        y_ref, sf_ref, sb_ref, hn_ref, gg_ref, xr_ref, a_ref, u_ref, h_ref = rest
    hp = pl.program_id(1)

    @pl.when(hp == 0)
    def _():
        hn_ref[...] = _norm_modulate(x_ref[...], ng_ref, mod_ref, 1).astype(BF16)

    wig = wig_ref[...].astype(BF16)
    wix = wix_ref[...].astype(BF16)
    seg_per_chunk = RG_CHUNK // SEG
    zero2 = jnp.zeros((2, LANES), F32)

    def project(c):
        rows = slice(c * RG_CHUNK, (c + 1) * RG_CHUNK)
        hn = hn_ref[rows, :]
        gg_ref[rows, :] = jax.nn.gelu(_dot(hn, wig))
        xr = _dot(hn, wix)
        for s in range(N_SLAB):
            lanes = slice(s * LANES, (s + 1) * LANES)
            for kk in range(seg_per_chunk):
                k = c * seg_per_chunk + kk
                base = _seg_base(k)
                seg = xr[kk * SEG:(kk + 1) * SEG, lanes]
                xr_ref[s, _seg_rows(k), :] = seg
                inside = chunked and k > 0
                xr_ref[s, base - SEG_GAP:base - 2, :] = seg[0:2, :] if inside else zero2
                inside = chunked and k < N_SEG - 1
                xr_ref[s, base + SEG + 2:base + SEG + SEG_GAP, :] = seg[SEG - 2:SEG, :] if inside else zero2
                if k == 0:
                    xr_ref[s, base - 2:base, :] = zero2
                if k == N_SEG - 1:
                    xr_ref[s, base + SEG:base + SEG + 2, :] = zero2

    z = -lam_ref[...]
    softplus = jnp.maximum(z, 0.0) + jnp.log1p(jnp.exp(-jnp.abs(z)))
    half_rate = (0.5 * (-RG_C)) * softplus
    half_ba = 0.5 * ba_ref[...]
    half_bx = 0.5 * bx_ref[...]
    cw = cw_ref[...]
    cb = cb_ref[...]

    def gates(k):
        taps = []
        for s in range(N_SLAB):
            lanes = slice(s * LANES, (s + 1) * LANES)
            acc = cb[:, lanes] + cw[0:1, lanes] * xr_ref[s, _seg_rows(k, -1), :]
            for j in range(1, CONV_W):
                acc += cw[j:j + 1, lanes] * xr_ref[s, _seg_rows(k, j - 1), :]
            taps.append(acc)
        xk = jnp.concatenate(taps, axis=1)
        pre = _dot(xk.astype(BF16), wg_ref[...])
        xh = 0.5 * xk
        for d in range(2):
            tr = jnp.tanh(pre[:, (2 * d) * HP_W:(2 * d + 1) * HP_W] + half_ba[d:d + 1, :])
            ti = jnp.tanh(pre[:, (2 * d + 1) * HP_W:(2 * d + 2) * HP_W] + half_bx[d:d + 1, :])
            a = jnp.exp(tr * half_rate[d:d + 1, :] + half_rate[d:d + 1, :])
            v = 1.0 - a * a
            beta = jnp.where(v > 0.0, v * lax.rsqrt(v), 0.0)
            u = beta * (xh * ti + xh)
            for s in range(N_SLAB):
                lanes = slice(s * LANES, (s + 1) * LANES)
                a_ref[d, s, _seg_rows(k), :] = a[:, lanes]
                u_ref[d, s, _seg_rows(k), :] = u[:, lanes]

    n_chunk = MIX_TT // RG_CHUNK
    project(0)
    for c in range(n_chunk):
        if c + 1 < n_chunk:
            project(c + 1)
        first = c * seg_per_chunk - 1
        for k in range(max(first, 0), first + seg_per_chunk):
            gates(k)
    gates(N_SEG - 1)

    def step(t, carry):
        hs, ps = carry
        new_h, new_p = [], []
        for d in range(2):
            row = SEG_GAP + (t if d == 0 else SEG - 1 - t)
            idx = pl.ds(row, N_SEG, stride=SEG_PITCH)
            for s in range(N_SLAB):
                a = a_ref[d, s, idx, :]
                h = a * hs[d * N_SLAB + s] + u_ref[d, s, idx, :]
                h_ref[d, s, idx, :] = h
                new_h.append(h)
                if chunked:
                    p = a * ps[d * N_SLAB + s]
                    p_ref[d, s, idx, :] = p
                    new_p.append(p)
        return tuple(new_h), tuple(new_p)

    if chunked:
        h_init = tuple(jnp.zeros((N_SEG, LANES), F32) for _ in range(2 * N_SLAB))
        p_init = tuple(jnp.ones((N_SEG, LANES), F32) for _ in range(2 * N_SLAB))
    else:
        h0 = (h0f_ref[...], h0b_ref[...])
        h_init = tuple(h0[d][:, s * LANES:(s + 1) * LANES] for d in range(2) for s in range(N_SLAB))
        p_init = ()
    h_fin, _ = lax.fori_loop(0, SEG, step, (h_init, p_init), unroll=8)

    if chunked:
        b = pl.program_id(0)
        h0 = (h0f_ref[pl.ds(b, 1), :], h0b_ref[pl.ds(b, 1), :])
        for s in range(N_SLAB):
            lanes = slice(s * LANES, (s + 1) * LANES)
            carry = h0[0][:, lanes]
            for k in range(N_SEG):
                h = h_ref[0, s, _seg_rows(k), :] + p_ref[0, s, _seg_rows(k), :] * carry
                h_ref[0, s, _seg_rows(k), :] = h
                carry = h[SEG - 1:SEG, :]
            carry = h0[1][:, lanes]
            for k in reversed(range(N_SEG)):
                hb = h_ref[1, s, _seg_rows(k), :] + p_ref[1, s, _seg_rows(k), :] * carry
                carry = hb[0:1, :]
                rows = slice(k * SEG, (k + 1) * SEG)
                y = (h_ref[0, s, _seg_rows(k), :] + hb) * gg_ref[rows, lanes]
                y_ref[rows, lanes] = y.astype(BF16)
    else:
        for s in range(N_SLAB):
            lanes = slice(s * LANES, (s + 1) * LANES)
            sf_ref[:, lanes] = h_fin[s]
            sb_ref[:, lanes] = h_fin[N_SLAB + s]
            for k in range(N_SEG):
                rows = slice(k * SEG, (k + 1) * SEG)
                y = (h_ref[0, s, _seg_rows(k), :] + h_ref[1, s, _seg_rows(k), :]) * gg_ref[rows, lanes]
                y_ref[rows, lanes] = y.astype(BF16)


def _rg_scan(x, mod, norm_g, h0f, h0b, rg, *, layer, j, chunked, row, tile0, n_tiles):
    t = n_tiles * MIX_TT
    w_in, conv_w, conv_b, w_gate, b_a, b_x, lam = rg
    n_state = h0f.shape[0]
    state_block = (n_state, HP_W) if chunked else (N_SEG, HP_W)
    state_map = (lambda i, hp: (0, hp)) if chunked else (lambda i, hp: (i, hp))
    in_specs = [
        pl.BlockSpec((MIX_TT, D_MODEL), lambda i, hp: (tile0 + i, 0)),
        pl.BlockSpec((None, None, N_MOD, D_MODEL), lambda i, hp: (layer, row(tile0 + i), 0, 0)),
        pl.BlockSpec((None, 6, D_MODEL), lambda i, hp: (layer, 0, 0)),
        pl.BlockSpec(state_block, state_map),
        pl.BlockSpec(state_block, state_map),
        pl.BlockSpec((None, D_MODEL, HP_W), lambda i, hp: (j, 0, hp)),
        pl.BlockSpec((None, D_MODEL, HP_W), lambda i, hp: (j, 0, N_HP + hp)),
        pl.BlockSpec((None, CONV_W, HP_W), lambda i, hp: (j, 0, hp)),
        pl.BlockSpec((None, 1, HP_W), lambda i, hp: (j, 0, hp)),
        pl.BlockSpec((None, None, HP_W, 4 * HP_W), lambda i, hp: (j, hp, 0, 0)),
        pl.BlockSpec((None, 2, HP_W), lambda i, hp: (j, 0, hp)),
        pl.BlockSpec((None, 2, HP_W), lambda i, hp: (j, 0, hp)),
        pl.BlockSpec((None, 2, HP_W), lambda i, hp: (j, 0, hp)),
    ]
    y_spec = pl.BlockSpec((MIX_TT, HP_W), lambda i, hp: (i, hp))
    y_shape = jax.ShapeDtypeStruct((t, D_RNN), BF16)
    if chunked:
        out_specs, out_shape = y_spec, y_shape
    else:
        s_spec = pl.BlockSpec((N_SEG, HP_W), lambda i, hp: (i, hp))
        s_shape = jax.ShapeDtypeStruct((t // SEG, D_RNN), F32)
        out_specs, out_shape = (y_spec, s_spec, s_spec), (y_shape, s_shape, s_shape)
    scan = pltpu.VMEM((2, N_SLAB, SCAN_ROWS, LANES), F32)
    scratch = [
        pltpu.VMEM((MIX_TT, D_MODEL), BF16),
        pltpu.VMEM((MIX_TT, HP_W), F32),
        pltpu.VMEM((N_SLAB, SCAN_ROWS, LANES), F32),
        scan, scan, scan,
    ] + ([scan] if chunked else [])
    return pl.pallas_call(
        functools.partial(_rg_kernel, chunked=chunked),
        grid=(t // MIX_TT, N_HP),
        in_specs=in_specs,
        out_specs=out_specs,
        out_shape=out_shape,
        scratch_shapes=scratch,
        compiler_params=pltpu.CompilerParams(
            dimension_semantics=("parallel", "arbitrary"), vmem_limit_bytes=VMEM_LIMIT),
        name="rg_scan",
    )(x, mod, norm_g, h0f, h0b, w_in, w_in, conv_w, conv_b, w_gate, b_a, b_x, lam)


def _block_diag_pairs(w):
    n = w.shape[0]
    w = w.reshape(n, 2, N_HP, 2, RNN_HEAD_DIM, RNN_HEAD_DIM)
    z = jnp.zeros_like(w[:, :, :, 0])
    top = jnp.concatenate([w[:, :, :, 0], z], axis=-1)
    bot = jnp.concatenate([z, w[:, :, :, 1]], axis=-1)
    return jnp.concatenate([top, bot], axis=-2)


def _pool_layout(on_grid):
    run, n_run, pad_runs = (GRID_W, MIX_TT // GRID_W, POOL_HALO) if on_grid else (SEG, N_SEG, 0)
    pitch = run + POOL_HALO
    rows = 2 * POOL_EDGE + POOL_HALO + (n_run + 2 * pad_runs) * pitch
    return run, n_run, pad_runs, pitch, rows


def _window_count(half, pos, length):
    return (jnp.minimum(pos + half, length) - jnp.maximum(pos - half, 0)).astype(F32)


def _pool_kernel(x_ref, mod_ref, ng_ref, pw_ref, ps_ref, o_ref, inv_ref, d_ref, xb_ref, b0_ref, b1_ref,
                 *, n_ctx):
    i = pl.program_id(0)
    g = pl.program_id(1)

    @pl.when(g == 0)
    def _():
        x = x_ref[...]
        inv_ref[...] = lax.rsqrt(jnp.mean(x * x, axis=-1, keepdims=True) + EPS)

    row_id = lax.broadcasted_iota(jnp.int32, (MIX_TT, 1), 0)
    zero_edge = jnp.zeros((POOL_EDGE, LANES), F32)

    def pool_group(gi, w, on_grid):
        run, n_run, pad_runs, pitch, n_rows = _pool_layout(on_grid)
        base = lambda r: POOL_EDGE + POOL_HALO + (r + pad_runs) * pitch
        levels = w.bit_length() - 1
        half = w // 2

        def window_sums(s):
            src, bufs = xb_ref, [b0_ref, b1_ref]
            lo, hi = POOL_EDGE, n_rows - POOL_EDGE
            for lvl in range(levels):
                dst = bufs[lvl % 2]
                dn, up = (1, 0) if lvl == 0 else (2 ** (lvl - 1),) * 2
                dst[s, lo:hi, :] = src[s, lo - dn:hi - dn, :] + src[s, lo + up:hi + up, :]
                src = dst
            if on_grid:
                reach = 0
                for lvl in range(levels):
                    dst = bufs[(levels + lvl) % 2]
                    dn, up = (1, 0) if lvl == 0 else (2 ** (lvl - 1),) * 2
                    reach += dn
                    lo, hi = base(reach - pad_runs), base(n_run + pad_runs - reach)
                    dst[s, lo:hi, :] = (src[s, lo - dn * pitch:hi - dn * pitch, :]
                                        + src[s, lo + up * pitch:hi + up * pitch, :])
                    src = dst
            return src

        cols = slice(gi * POOL_GROUP_DIM, (gi + 1) * POOL_GROUP_DIM)
        scale = ng_ref[2:3, cols] * (1.0 + mod_ref[4:5, cols])
        xg = (x_ref[:, cols] * inv_ref[...]) * scale + mod_ref[3:4, cols]
        if on_grid:
            cnt = (_window_count(half, row_id % GRID_W, GRID_W)
                   * _window_count(half, row_id // GRID_W, n_run))
        else:
            cnt = _window_count(half, row_id % SEG, SEG)
        inv_cnt = 1.0 / cnt
        for s in range(POOL_GROUP_DIM // LANES):
            lanes = slice(s * LANES, (s + 1) * LANES)
            xb_ref[s, 0:n_rows, :] = jnp.zeros((n_rows, LANES), F32)
            for r in range(n_run):
                xb_ref[s, base(r):base(r) + run, :] = xg[r * run:(r + 1) * run, lanes]
            for buf in (b0_ref, b1_ref):
                buf[s, 0:POOL_EDGE, :] = zero_edge
                buf[s, n_rows - POOL_EDGE:n_rows, :] = zero_edge
            tot = window_sums(s)
            for r in range(n_run):
                rows = slice(r * run, (r + 1) * run)
                mean = tot[s, base(r):base(r) + run, :] * inv_cnt[rows, :]
                d_ref[rows, lanes] = (mean - xb_ref[s, base(r):base(r) + run, :]).astype(BF16)
        o_ref[:, cols] = _dot(d_ref[...], pw_ref[...].astype(BF16)) * ps_ref[...]

    for on_grid in (False, True):
        for gi, w in enumerate(POOL_WINDOWS):
            pl.when((g == gi) & ((i >= n_ctx) == on_grid))(functools.partial(pool_group, gi, w, on_grid))

    @pl.when(g == POOL_GROUPS - 1)
    def _():
        gain = mod_ref[5:6, :] * ng_ref[3:4, :]
        for r in range(MIX_TT // SEG):
            rows = slice(r * SEG, (r + 1) * SEG)
            y = o_ref[rows, :]
            inv = lax.rsqrt(jnp.mean(y * y, axis=-1, keepdims=True) + EPS)
            o_ref[rows, :] = x_ref[rows, :] + (y * inv) * gain


def _pool_mix(x, mod, norm_g, pool_w, pool_scale, *, layer, j, row, n_ctx):
    t = x.shape[0]
    n_rows = max(_pool_layout(False)[-1], _pool_layout(True)[-1])
    buf = pltpu.VMEM((POOL_GROUP_DIM // LANES, n_rows, LANES), F32)
    return pl.pallas_call(
        functools.partial(_pool_kernel, n_ctx=n_ctx),
        grid=(t // MIX_TT, POOL_GROUPS),
        in_specs=[
            pl.BlockSpec((MIX_TT, D_MODEL), lambda i, g: (i, 0)),
            pl.BlockSpec((None, None, N_MOD, D_MODEL), lambda i, g: (layer, row(i), 0, 0)),
            pl.BlockSpec((None, 6, D_MODEL), lambda i, g: (layer, 0, 0)),
            pl.BlockSpec((None, None, POOL_GROUP_DIM, POOL_GROUP_DIM), lambda i, g: (j, g, 0, 0)),
            pl.BlockSpec((None, 1, POOL_GROUP_DIM), lambda i, g: (j, 0, g)),
        ],
        out_specs=pl.BlockSpec((MIX_TT, D_MODEL), lambda i, g: (i, 0)),
        out_shape=jax.ShapeDtypeStruct((t, D_MODEL), F32),
        scratch_shapes=[
            pltpu.VMEM((MIX_TT, 1), F32),
            pltpu.VMEM((MIX_TT, POOL_GROUP_DIM), BF16),
            buf, buf, buf,
        ],
        compiler_params=pltpu.CompilerParams(
            dimension_semantics=("parallel", "arbitrary"), vmem_limit_bytes=VMEM_LIMIT),
        name="pool_mix",
    )(x, mod, norm_g, pool_w, pool_scale)


def kernel(x_prompt, x_sample, state_rglru, c, c_ctx, mod_w, mod_b, norm_g, ffn_w1, ffn_w3, ffn_w2,
           rg_w_in, rg_conv_w, rg_conv_b, rg_w_a, rg_b_a, rg_w_x, rg_b_x, rg_lam, rg_w_out,
           pool_w, pool_scale):
    batch, seq, _ = x_prompt.shape
    dec_batch, dec_seq, _ = x_sample.shape
    ctx_rows, lat_rows = batch * seq, dec_batch * dec_seq
    assert seq == SEG and dec_seq == MIX_TT and dec_seq % GRID_W == 0
    assert ctx_rows % MIX_TT == 0 and MIX_TT % FFN_TM == 0 and FFN_TM == RES_TM
    assert 1 + dec_batch <= MOD_ROWS
    n_rg = rg_w_in.shape[0]

    cond = jnp.zeros((MOD_ROWS, D_MODEL), F32).at[0].set(c_ctx).at[1:1 + dec_batch].set(c)
    mod = _modulation(cond, mod_w, mod_b).reshape(DEPTH, MOD_ROWS, N_MOD, D_MODEL)

    wa, wx = _block_diag_pairs(rg_w_a), _block_diag_pairs(rg_w_x)
    w_gate = (0.5 * jnp.concatenate([wa[:, 0], wx[:, 0], wa[:, 1], wx[:, 1]], axis=-1)).astype(BF16)
    rg = (rg_w_in, rg_conv_w, rg_conv_b.reshape(n_rg, 1, D_RNN), w_gate, rg_b_a, rg_b_x, rg_lam)
    pool_scale3 = pool_scale.reshape(pool_scale.shape[0], 1, D_MODEL)
    zeros = jnp.zeros((batch, D_RNN), F32)
    ffn_w1, ffn_w3, ffn_w2 = (w.astype(BF16) for w in (ffn_w1, ffn_w3, ffn_w2))

    x = jnp.concatenate([x_prompt.reshape(ctx_rows, D_MODEL), x_sample.reshape(lat_rows, D_MODEL)], axis=0)
    ffn_row = _mod_row_map(FFN_TM, ctx_rows, dec_seq)
    mix_row = _mod_row_map(MIX_TT, ctx_rows, dec_seq)
    n_ffn_ctx, n_ffn_lat = ctx_rows // FFN_TM, lat_rows // FFN_TM
    n_mix_ctx, n_mix_lat = ctx_rows // MIX_TT, lat_rows // MIX_TT
    ffn = functools.partial(_ffn, mod=mod, norm_g=norm_g, w1=ffn_w1, w3=ffn_w3, w2=ffn_w2, row=ffn_row)

    ctx_states = []
    for l in range(DEPTH):
        j = l // 2
        x = ffn(x, layer=l, stage=0, tile0=0, n_tiles=n_ffn_ctx + n_ffn_lat)
        if l % 2 == 0:
            scan = functools.partial(_rg_scan, x, mod, norm_g, rg=rg, layer=l, j=j, row=mix_row)
            y_ctx, sf, sb = scan(zeros, zeros, chunked=False, tile0=0, n_tiles=n_mix_ctx)
            y_lat = scan(state_rglru[:, j, 0], state_rglru[:, j, 1], chunked=True,
                         tile0=n_mix_ctx, n_tiles=n_mix_lat)
            ctx_states.append(jnp.stack([sf, sb], axis=1))
            x = _out_proj(x, y_ctx, y_lat, mod, norm_g, rg_w_out, layer=l, j=j, row=ffn_row)
        else:
            x = _pool_mix(x, mod, norm_g, pool_w, pool_scale3, layer=l, j=j, row=mix_row, n_ctx=n_mix_ctx)
        if l + 1 < DEPTH:
            x = ffn(x, layer=l, stage=2, tile0=0, n_tiles=n_ffn_ctx + n_ffn_lat)
    y_prompt = ffn(x, layer=DEPTH - 1, stage=2, tile0=0, n_tiles=n_ffn_ctx)
    y_sample = ffn(x, layer=DEPTH - 1, stage=2, tile0=n_ffn_ctx, n_tiles=n_ffn_lat)
    new_state = jnp.stack(ctx_states, axis=1)
    return (y_prompt.reshape(batch, seq, D_MODEL), y_sample.reshape(dec_batch, dec_seq, D_MODEL), new_state)
```

```python
import functools

import jax
import jax.numpy as jnp
from jax import lax
from jax.experimental import pallas as pl
from jax.experimental.pallas import tpu as pltpu

D_MODEL = 1024
DEPTH = 4
D_RNN = 1024
RNN_HEADS = 8
RNN_HEAD_DIM = D_RNN // RNN_HEADS
CONV_W = 4
RG_C = 8.0
GRID_W = 64
POOL_WINDOWS = (2, 4, 8, 16)
POOL_GROUPS = 4
POOL_GROUP_DIM = D_MODEL // POOL_GROUPS
D_FF = 2816
N_MOD = 9
EPS = 1e-6

F32 = jnp.float32
BF16 = jnp.bfloat16

MOD_ROWS = 8
MOD_TN = 1024
FFN_TM = 1024
FFN_TF = 256
FFN_TN = 256
FFN_RB = 256
FFN_NA = D_FF // FFN_TF
FFN_STAGE_SLOTS = 2
FFN_CH_UP = 128
FFN_CH_DN = 352
FFN_NB = D_MODEL // FFN_TN
RES_TM = 1024
SEG = 256
N_SEG = 8
MIX_TT = SEG * N_SEG
SEG_GAP = 4
SEG_PITCH = SEG + SEG_GAP
SCAN_ROWS = SEG_GAP + N_SEG * SEG_PITCH + 4
RG_CHUNK = 2 * SEG
HP_W = 2 * RNN_HEAD_DIM
N_HP = D_RNN // HP_W
LANES = 128
N_SLAB = HP_W // LANES
POOL_HALO = max(POOL_WINDOWS) // 2
POOL_EDGE = 8
VMEM_LIMIT = 60 * 1024 * 1024


def _dot(a, b):
    return jnp.dot(a, b, preferred_element_type=F32)


def _rms(x, g):
    ms = jnp.mean(x * x, axis=-1, keepdims=True)
    return x * lax.rsqrt(ms + EPS) * g


def _sigmoid(x):
    return 0.5 * jnp.tanh(0.5 * x) + 0.5


def _norm_modulate(x, ng_ref, mod_ref, stage):
    g = ng_ref[2 * stage:2 * stage + 1, :]
    shift = mod_ref[3 * stage:3 * stage + 1, :]
    scale = mod_ref[3 * stage + 1:3 * stage + 2, :]
    inv = lax.rsqrt(jnp.mean(x * x, axis=-1, keepdims=True) + EPS)
    return (x * inv) * (g * (1.0 + scale)) + shift


def _mod_row_map(tile_rows, ctx_rows, lat_rows):
    def row(tile):
        start = tile * tile_rows
        return jnp.where(start < ctx_rows, 0, 1 + (start - ctx_rows) // lat_rows)
    return row


def _mod_kernel(c_ref, w_ref, b_ref, o_ref):
    c = c_ref[...]
    s = c * _sigmoid(c)
    o_ref[...] = _dot(s.astype(BF16), w_ref[...].astype(BF16)) + b_ref[...]


def _modulation(cond, mod_w, mod_b):
    tn = MOD_TN
    n_col = (N_MOD * D_MODEL) // tn
    return pl.pallas_call(
        _mod_kernel,
        grid=(DEPTH, n_col),
        in_specs=[
            pl.BlockSpec((MOD_ROWS, D_MODEL), lambda l, n: (0, 0)),
            pl.BlockSpec((None, D_MODEL, tn), lambda l, n: (l, 0, n)),
            pl.BlockSpec((None, 1, tn), lambda l, n: (l, 0, n)),
        ],
        out_specs=pl.BlockSpec((None, MOD_ROWS, tn), lambda l, n: (l, 0, n)),
        out_shape=jax.ShapeDtypeStruct((DEPTH, MOD_ROWS, N_MOD * D_MODEL), F32),
        compiler_params=pltpu.CompilerParams(
            dimension_semantics=("arbitrary", "arbitrary"), vmem_limit_bytes=VMEM_LIMIT),
        name="modulation",
    )(cond, mod_w, mod_b.reshape(DEPTH, 1, N_MOD * D_MODEL))


def _stage_weights(streams, stage_ref, sem_ref, chunk):
    jobs = [(src, dst, c * chunk) for src, dst in streams for c in range(src.shape[0] // chunk)]

    def copy(n):
        src, _, row0 = jobs[n]
        slot = n % FFN_STAGE_SLOTS
        return pltpu.make_async_copy(src.at[pl.ds(row0, chunk), :], stage_ref.at[slot], sem_ref.at[slot])

    for n in range(min(FFN_STAGE_SLOTS, len(jobs))):
        copy(n).start()
    for n, (_, dst, row0) in enumerate(jobs):
        copy(n).wait()
        dst[row0:row0 + chunk, :] = stage_ref[n % FFN_STAGE_SLOTS].astype(BF16)
        if n + FFN_STAGE_SLOTS < len(jobs):
            copy(n + FFN_STAGE_SLOTS).start()


def _ffn_kernel(x_ref, xn_ref, mod_ref, modn_ref, ng_ref, w1_hbm, w3_hbm, w2_hbm, o_ref,
                hn_ref, p_ref, w1_ref, w3_ref, w2_ref, up_stage, dn_stage, up_sem, dn_sem,
                *, layer, sub, stage):
    @pl.when(pl.program_id(0) == 0)
    def _():
        _stage_weights([(w1_hbm.at[layer, sub], w1_ref), (w3_hbm.at[layer, sub], w3_ref)],
                       up_stage, up_sem, FFN_CH_UP)
        _stage_weights([(w2_hbm.at[layer, sub], w2_ref)], dn_stage, dn_sem, FFN_CH_DN)
        hn_ref[...] = _norm_modulate(x_ref[...], ng_ref, mod_ref, stage).astype(BF16)

    gain = ng_ref[2 * stage + 1:2 * stage + 2, :] * (0.5 * mod_ref[3 * stage + 2:3 * stage + 3, :])
    n_blocks = FFN_TM // FFN_RB
    block_rows = [slice(r * FFN_RB, (r + 1) * FFN_RB) for r in range(n_blocks)]

    def up_project(rows):
        for k in range(FFN_NA):
            cols = slice(k * FFN_TF, (k + 1) * FFN_TF)
            a = _dot(hn_ref[rows, :], w1_ref[:, cols])
            b = _dot(hn_ref[rows, :], w3_ref[:, cols])
            p_ref[k, rows, :] = (a * _sigmoid(a) * b).astype(BF16)

    def down_project(r):
        rows = block_rows[r]
        piece = FFN_RB // FFN_NB
        accs = []
        for n in range(FFN_NB):
            nrows = slice(r * FFN_RB + n * piece, r * FFN_RB + (n + 1) * piece)
            hn_ref[nrows, :] = _norm_modulate(xn_ref[nrows, :], ng_ref, modn_ref, stage).astype(BF16)
            cols = slice(n * FFN_TN, (n + 1) * FFN_TN)
            acc = _dot(p_ref[0, rows, :], w2_ref[0:FFN_TF, cols])
            for k in range(1, FFN_NA):
                acc += _dot(p_ref[k, rows, :], w2_ref[k * FFN_TF:(k + 1) * FFN_TF, cols])
            accs.append(acc)
        ss = sum(jnp.sum(acc * acc, axis=-1, keepdims=True) for acc in accs)
        inv = lax.rsqrt(ss * (1.0 / D_MODEL) + EPS)
        for n, acc in enumerate(accs):
            cols = slice(n * FFN_TN, (n + 1) * FFN_TN)
            o_ref[rows, cols] = x_ref[rows, cols] + (acc * inv) * gain[:, cols]

    up_project(block_rows[0])
    for r in range(n_blocks):
        if r + 1 < n_blocks:
            up_project(block_rows[r + 1])
        down_project(r)


def _ffn(x, mod, norm_g, w1, w3, w2, *, layer, stage, row, tile0, n_tiles):
    sub = 0 if stage == 0 else 1
    cur = lambda i: tile0 + i
    nxt = lambda i: tile0 + jnp.minimum(i + 1, n_tiles - 1)
    return pl.pallas_call(
        functools.partial(_ffn_kernel, layer=layer, sub=sub, stage=stage),
        grid=(n_tiles,),
        in_specs=[
            pl.BlockSpec((FFN_TM, D_MODEL), lambda i: (cur(i), 0)),
            pl.BlockSpec((FFN_TM, D_MODEL), lambda i: (nxt(i), 0)),
            pl.BlockSpec((None, None, N_MOD, D_MODEL), lambda i: (layer, row(cur(i)), 0, 0)),
            pl.BlockSpec((None, None, N_MOD, D_MODEL), lambda i: (layer, row(nxt(i)), 0, 0)),
            pl.BlockSpec((None, 6, D_MODEL), lambda i: (layer, 0, 0)),
            pl.BlockSpec(memory_space=pl.ANY),
            pl.BlockSpec(memory_space=pl.ANY),
            pl.BlockSpec(memory_space=pl.ANY),
        ],
        out_specs=pl.BlockSpec((FFN_TM, D_MODEL), lambda i: (i, 0)),
        out_shape=jax.ShapeDtypeStruct((n_tiles * FFN_TM, D_MODEL), F32),
        scratch_shapes=[
            pltpu.VMEM((FFN_TM, D_MODEL), BF16),
            pltpu.VMEM((FFN_NA, FFN_TM, FFN_TF), BF16),
            pltpu.VMEM((D_MODEL, D_FF), BF16),
            pltpu.VMEM((D_MODEL, D_FF), BF16),
            pltpu.VMEM((D_FF, D_MODEL), BF16),
            pltpu.VMEM((FFN_STAGE_SLOTS, FFN_CH_UP, D_FF), F32),
            pltpu.VMEM((FFN_STAGE_SLOTS, FFN_CH_DN, D_MODEL), F32),
            pltpu.SemaphoreType.DMA((FFN_STAGE_SLOTS,)),
            pltpu.SemaphoreType.DMA((FFN_STAGE_SLOTS,)),
        ],
        compiler_params=pltpu.CompilerParams(
            dimension_semantics=("arbitrary",), vmem_limit_bytes=VMEM_LIMIT),
        name="ffn",
    )(x, x, mod, mod, norm_g, w1, w3, w2)


def _out_proj_kernel(x_ref, yc_ref, yl_ref, mod_ref, ng_ref, w_ref, o_ref, *, n_ctx):
    y_in = jnp.where(pl.program_id(0) < n_ctx, yc_ref[...], yl_ref[...])
    y = _dot(y_in, w_ref[...].astype(BF16))
    o_ref[...] = x_ref[...] + mod_ref[5:6, :] * _rms(y, ng_ref[3:4, :])


def _out_proj(x, y_ctx, y_lat, mod, norm_g, w_out, *, layer, j, row):
    n_ctx = y_ctx.shape[0] // RES_TM
    n_lat = y_lat.shape[0] // RES_TM
    return pl.pallas_call(
        functools.partial(_out_proj_kernel, n_ctx=n_ctx),
        grid=(n_ctx + n_lat,),
        in_specs=[
            pl.BlockSpec((RES_TM, D_MODEL), lambda i: (i, 0)),
            pl.BlockSpec((RES_TM, D_RNN), lambda i: (jnp.minimum(i, n_ctx - 1), 0)),
            pl.BlockSpec((RES_TM, D_RNN), lambda i: (jnp.maximum(i - n_ctx, 0), 0)),
            pl.BlockSpec((None, None, N_MOD, D_MODEL), lambda i: (layer, row(i), 0, 0)),
            pl.BlockSpec((None, 6, D_MODEL), lambda i: (layer, 0, 0)),
            pl.BlockSpec((None, D_RNN, D_MODEL), lambda i: (j, 0, 0)),
        ],
        out_specs=pl.BlockSpec((RES_TM, D_MODEL), lambda i: (i, 0)),
        out_shape=jax.ShapeDtypeStruct(x.shape, F32),
        compiler_params=pltpu.CompilerParams(
            dimension_semantics=("parallel",), vmem_limit_bytes=VMEM_LIMIT),
        name="out_proj",
    )(x, y_ctx, y_lat, mod, norm_g, w_out)


def _seg_base(k):
    return SEG_GAP + k * SEG_PITCH


def _seg_rows(k, offset=0):
    return pl.ds(_seg_base(k) + offset, SEG)


def _rg_kernel(x_ref, mod_ref, ng_ref, h0f_ref, h0b_ref, wig_ref, wix_ref, cw_ref, cb_ref, wg_ref,
               ba_ref, bx_ref, lam_ref, *rest, chunked):
    if chunked:
        y_ref, hn_ref, gg_ref, xr_ref, a_ref, u_ref, h_ref, p_ref = rest
    else:
        y_ref, sf_ref, sb_ref, hn_ref, gg_ref, xr_ref, a_ref, u_ref, h_ref = rest
    hp = pl.program_id(1)

    @pl.when(hp == 0)
    def _():
        hn_ref[...] = _norm_modulate(x_ref[...], ng_ref, mod_ref, 1).astype(BF16)

    wig = wig_ref[...].astype(BF16)
    wix = wix_ref[...].astype(BF16)
    seg_per_chunk = RG_CHUNK // SEG
    zero2 = jnp.zeros((2, LANES), F32)

    def project(c):
        rows = slice(c * RG_CHUNK, (c + 1) * RG_CHUNK)
        hn = hn_ref[rows, :]
        gg_ref[rows, :] = jax.nn.gelu(_dot(hn, wig))
        xr = _dot(hn, wix)
        for s in range(N_SLAB):
            lanes = slice(s * LANES, (s + 1) * LANES)
            for kk in range(seg_per_chunk):
                k = c * seg_per_chunk + kk
                base = _seg_base(k)
                seg = xr[kk * SEG:(kk + 1) * SEG, lanes]
                xr_ref[s, _seg_rows(k), :] = seg
                inside = chunked and k > 0
                xr_ref[s, base - SEG_GAP:base - 2, :] = seg[0:2, :] if inside else zero2
                inside = chunked and k < N_SEG - 1
                xr_ref[s, base + SEG + 2:base + SEG + SEG_GAP, :] = seg[SEG - 2:SEG, :] if inside else zero2
                if k == 0:
                    xr_ref[s, base - 2:base, :] = zero2
                if k == N_SEG - 1:
                    xr_ref[s, base + SEG:base + SEG + 2, :] = zero2

    z = -lam_ref[...]
    softplus = jnp.maximum(z, 0.0) + jnp.log1p(jnp.exp(-jnp.abs(z)))
    half_rate = (0.5 * (-RG_C)) * softplus
    half_ba = 0.5 * ba_ref[...]
    half_bx = 0.5 * bx_ref[...]
    cw = cw_ref[...]
    cb = cb_ref[...]

    def gates(k):
        taps = []
        for s in range(N_SLAB):
            lanes = slice(s * LANES, (s + 1) * LANES)
            acc = cb[:, lanes] + cw[0:1, lanes] * xr_ref[s, _seg_rows(k, -1), :]
            for j in range(1, CONV_W):
                acc += cw[j:j + 1, lanes] * xr_ref[s, _seg_rows(k, j - 1), :]
            taps.append(acc)
        xk = jnp.concatenate(taps, axis=1)
        pre = _dot(xk.astype(BF16), wg_ref[...])
        xh = 0.5 * xk
        for d in range(2):
            tr = jnp.tanh(pre[:, (2 * d) * HP_W:(2 * d + 1) * HP_W] + half_ba[d:d + 1, :])
            ti = jnp.tanh(pre[:, (2 * d + 1) * HP_W:(2 * d + 2) * HP_W] + half_bx[d:d + 1, :])
            a = jnp.exp(tr * half_rate[d:d + 1, :] + half_rate[d:d + 1, :])
            v = 1.0 - a * a
            beta = jnp.where(v > 0.0, v * lax.rsqrt(v), 0.0)
            u = beta * (xh * ti + xh)
            for s in range(N_SLAB):
                lanes = slice(s * LANES, (s + 1) * LANES)
                a_ref[d, s, _seg_rows(k), :] = a[:, lanes]
                u_ref[d, s, _seg_rows(k), :] = u[:, lanes]

    n_chunk = MIX_TT // RG_CHUNK
    project(0)
    for c in range(n_chunk):
        if c + 1 < n_chunk:
            project(c + 1)
        first = c * seg_per_chunk - 1
        for k in range(max(first, 0), first + seg_per_chunk):
            gates(k)
    gates(N_SEG - 1)

    def step(t, carry):
        hs, ps = carry
        new_h, new_p = [], []
        for d in range(2):
            row = SEG_GAP + (t if d == 0 else SEG - 1 - t)
            idx = pl.ds(row, N_SEG, stride=SEG_PITCH)
            for s in range(N_SLAB):
                a = a_ref[d, s, idx, :]
                h = a * hs[d * N_SLAB + s] + u_ref[d, s, idx, :]
                h_ref[d, s, idx, :] = h
                new_h.append(h)
                if chunked:
                    p = a * ps[d * N_SLAB + s]
                    p_ref[d, s, idx, :] = p
                    new_p.append(p)
        return tuple(new_h), tuple(new_p)

    if chunked:
        h_init = tuple(jnp.zeros((N_SEG, LANES), F32) for _ in range(2 * N_SLAB))
        p_init = tuple(jnp.ones((N_SEG, LANES), F32) for _ in range(2 * N_SLAB))
    else:
        h0 = (h0f_ref[...], h0b_ref[...])
        h_init = tuple(h0[d][:, s * LANES:(s + 1) * LANES] for d in range(2) for s in range(N_SLAB))
        p_init = ()
    h_fin, _ = lax.fori_loop(0, SEG, step, (h_init, p_init), unroll=8)

    if chunked:
        b = pl.program_id(0)
        h0 = (h0f_ref[pl.ds(b, 1), :], h0b_ref[pl.ds(b, 1), :])
        for s in range(N_SLAB):
            lanes = slice(s * LANES, (s + 1) * LANES)
            carry = h0[0][:, lanes]
            for k in range(N_SEG):
                h = h_ref[0, s, _seg_rows(k), :] + p_ref[0, s, _seg_rows(k), :] * carry
                h_ref[0, s, _seg_rows(k), :] = h
                carry = h[SEG - 1:SEG, :]
            carry = h0[1][:, lanes]
            for k in reversed(range(N_SEG)):
                hb = h_ref[1, s, _seg_rows(k), :] + p_ref[1, s, _seg_rows(k), :] * carry
                carry = hb[0:1, :]
                rows = slice(k * SEG, (k + 1) * SEG)
                y = (h_ref[0, s, _seg_rows(k), :] + hb) * gg_ref[rows, lanes]
                y_ref[rows, lanes] = y.astype(BF16)
    else:
        for s in range(N_SLAB):
            lanes = slice(s * LANES, (s + 1) * LANES)
            sf_ref[:, lanes] = h_fin[s]
            sb_ref[:, lanes] = h_fin[N_SLAB + s]
            for k in range(N_SEG):
                rows = slice(k * SEG, (k + 1) * SEG)
                y = (h_ref[0, s, _seg_rows(k), :] + h_ref[1, s, _seg_rows(k), :]) * gg_ref[rows, lanes]
                y_ref[rows, lanes] = y.astype(BF16)


def _rg_scan(x, mod, norm_g, h0f, h0b, rg, *, layer, j, chunked, row, tile0, n_tiles):
    t = n_tiles * MIX_TT
    w_in, conv_w, conv_b, w_gate, b_a, b_x, lam = rg
    n_state = h0f.shape[0]
    state_block = (n_state, HP_W) if chunked else (N_SEG, HP_W)
    state_map = (lambda i, hp: (0, hp)) if chunked else (lambda i, hp: (i, hp))
    in_specs = [
        pl.BlockSpec((MIX_TT, D_MODEL), lambda i, hp: (tile0 + i, 0)),
        pl.BlockSpec((None, None, N_MOD, D_MODEL), lambda i, hp: (layer, row(tile0 + i), 0, 0)),
        pl.BlockSpec((None, 6, D_MODEL), lambda i, hp: (layer, 0, 0)),
        pl.BlockSpec(state_block, state_map),
        pl.BlockSpec(state_block, state_map),
        pl.BlockSpec((None, D_MODEL, HP_W), lambda i, hp: (j, 0, hp)),
        pl.BlockSpec((None, D_MODEL, HP_W), lambda i, hp: (j, 0, N_HP + hp)),
        pl.BlockSpec((None, CONV_W, HP_W), lambda i, hp: (j, 0, hp)),
        pl.BlockSpec((None, 1, HP_W), lambda i, hp: (j, 0, hp)),
        pl.BlockSpec((None, None, HP_W, 4 * HP_W), lambda i, hp: (j, hp, 0, 0)),
        pl.BlockSpec((None, 2, HP_W), lambda i, hp: (j, 0, hp)),
        pl.BlockSpec((None, 2, HP_W), lambda i, hp: (j, 0, hp)),
        pl.BlockSpec((None, 2, HP_W), lambda i, hp: (j, 0, hp)),
    ]
    y_spec = pl.BlockSpec((MIX_TT, HP_W), lambda i, hp: (i, hp))
    y_shape = jax.ShapeDtypeStruct((t, D_RNN), BF16)
    if chunked:
        out_specs, out_shape = y_spec, y_shape
    else:
        s_spec = pl.BlockSpec((N_SEG, HP_W), lambda i, hp: (i, hp))
        s_shape = jax.ShapeDtypeStruct((t // SEG, D_RNN), F32)
        out_specs, out_shape = (y_spec, s_spec, s_spec), (y_shape, s_shape, s_shape)
    scan = pltpu.VMEM((2, N_SLAB, SCAN_ROWS, LANES), F32)
    scratch = [
        pltpu.VMEM((MIX_TT, D_MODEL), BF16),
        pltpu.VMEM((MIX_TT, HP_W), F32),
        pltpu.VMEM((N_SLAB, SCAN_ROWS, LANES), F32),
        scan, scan, scan,
    ] + ([scan] if chunked else [])
    return pl.pallas_call(
        functools.partial(_rg_kernel, chunked=chunked),
        grid=(t // MIX_TT, N_HP),
        in_specs=in_specs,
        out_specs=out_specs,
        out_shape=out_shape,
        scratch_shapes=scratch,
        compiler_params=pltpu.CompilerParams(
            dimension_semantics=("parallel", "arbitrary"), vmem_limit_bytes=VMEM_LIMIT),
        name="rg_scan",
    )(x, mod, norm_g, h0f, h0b, w_in, w_in, conv_w, conv_b, w_gate, b_a, b_x, lam)


def _block_diag_pairs(w):
    n = w.shape[0]
    w = w.reshape(n, 2, N_HP, 2, RNN_HEAD_DIM, RNN_HEAD_DIM)
    z = jnp.zeros_like(w[:, :, :, 0])
    top = jnp.concatenate([w[:, :, :, 0], z], axis=-1)
    bot = jnp.concatenate([z, w[:, :, :, 1]], axis=-1)
    return jnp.concatenate([top, bot], axis=-2)


def _pool_layout(on_grid):
    run, n_run, pad_runs = (GRID_W, MIX_TT // GRID_W, POOL_HALO) if on_grid else (SEG, N_SEG, 0)
    pitch = run + POOL_HALO
    rows = 2 * POOL_EDGE + POOL_HALO + (n_run + 2 * pad_runs) * pitch
    return run, n_run, pad_runs, pitch, rows


def _window_count(half, pos, length):
    return (jnp.minimum(pos + half, length) - jnp.maximum(pos - half, 0)).astype(F32)


def _pool_kernel(x_ref, mod_ref, ng_ref, pw_ref, ps_ref, o_ref, inv_ref, d_ref, xb_ref, b0_ref, b1_ref,
                 *, n_ctx):
    i = pl.program_id(0)
    g = pl.program_id(1)

    @pl.when(g == 0)
    def _():
        x = x_ref[...]
        inv_ref[...] = lax.rsqrt(jnp.mean(x * x, axis=-1, keepdims=True) + EPS)

    row_id = lax.broadcasted_iota(jnp.int32, (MIX_TT, 1), 0)
    zero_edge = jnp.zeros((POOL_EDGE, LANES), F32)

    def pool_group(gi, w, on_grid):
        run, n_run, pad_runs, pitch, n_rows = _pool_layout(on_grid)
        base = lambda r: POOL_EDGE + POOL_HALO + (r + pad_runs) * pitch
        levels = w.bit_length() - 1
        half = w // 2

        def window_sums(s):
            src, bufs = xb_ref, [b0_ref, b1_ref]
            lo, hi = POOL_EDGE, n_rows - POOL_EDGE
            for lvl in range(levels):
                dst = bufs[lvl % 2]
                dn, up = (1, 0) if lvl == 0 else (2 ** (lvl - 1),) * 2
                dst[s, lo:hi, :] = src[s, lo - dn:hi - dn, :] + src[s, lo + up:hi + up, :]
                src = dst
            if on_grid:
                reach = 0
                for lvl in range(levels):
                    dst = bufs[(levels + lvl) % 2]
                    dn, up = (1, 0) if lvl == 0 else (2 ** (lvl - 1),) * 2
                    reach += dn
                    lo, hi = base(reach - pad_runs), base(n_run + pad_runs - reach)
                    dst[s, lo:hi, :] = (src[s, lo - dn * pitch:hi - dn * pitch, :]
                                        + src[s, lo + up * pitch:hi + up * pitch, :])
                    src = dst
            return src

        cols = slice(gi * POOL_GROUP_DIM, (gi + 1) * POOL_GROUP_DIM)
        scale = ng_ref[2:3, cols] * (1.0 + mod_ref[4:5, cols])
        xg = (x_ref[:, cols] * inv_ref[...]) * scale + mod_ref[3:4, cols]
        if on_grid:
            cnt = (_window_count(half, row_id % GRID_W, GRID_W)
                   * _window_count(half, row_id // GRID_W, n_run))
        else:
            cnt = _window_count(half, row_id % SEG, SEG)
        inv_cnt = 1.0 / cnt
        for s in range(POOL_GROUP_DIM // LANES):
            lanes = slice(s * LANES, (s + 1) * LANES)
            xb_ref[s, 0:n_rows, :] = jnp.zeros((n_rows, LANES), F32)
            for r in range(n_run):
                xb_ref[s, base(r):base(r) + run, :] = xg[r * run:(r + 1) * run, lanes]
            for buf in (b0_ref, b1_ref):
                buf[s, 0:POOL_EDGE, :] = zero_edge
                buf[s, n_rows - POOL_EDGE:n_rows, :] = zero_edge
            tot = window_sums(s)
            for r in range(n_run):
                rows = slice(r * run, (r + 1) * run)
                mean = tot[s, base(r):base(r) + run, :] * inv_cnt[rows, :]
                d_ref[rows, lanes] = (mean - xb_ref[s, base(r):base(r) + run, :]).astype(BF16)
        o_ref[:, cols] = _dot(d_ref[...], pw_ref[...].astype(BF16)) * ps_ref[...]

    for on_grid in (False, True):
        for gi, w in enumerate(POOL_WINDOWS):
            pl.when((g == gi) & ((i >= n_ctx) == on_grid))(functools.partial(pool_group, gi, w, on_grid))

    @pl.when(g == POOL_GROUPS - 1)
    def _():
        gain = mod_ref[5:6, :] * ng_ref[3:4, :]
        for r in range(MIX_TT // SEG):
            rows = slice(r * SEG, (r + 1) * SEG)
            y = o_ref[rows, :]
            inv = lax.rsqrt(jnp.mean(y * y, axis=-1, keepdims=True) + EPS)
            o_ref[rows, :] = x_ref[rows, :] + (y * inv) * gain


def _pool_mix(x, mod, norm_g, pool_w, pool_scale, *, layer, j, row, n_ctx):
    t = x.shape[0]
    n_rows = max(_pool_layout(False)[-1], _pool_layout(True)[-1])
    buf = pltpu.VMEM((POOL_GROUP_DIM // LANES, n_rows, LANES), F32)
    return pl.pallas_call(
        functools.partial(_pool_kernel, n_ctx=n_ctx),
        grid=(t // MIX_TT, POOL_GROUPS),
        in_specs=[
            pl.BlockSpec((MIX_TT, D_MODEL), lambda i, g: (i, 0)),
            pl.BlockSpec((None, None, N_MOD, D_MODEL), lambda i, g: (layer, row(i), 0, 0)),
            pl.BlockSpec((None, 6, D_MODEL), lambda i, g: (layer, 0, 0)),
            pl.BlockSpec((None, None, POOL_GROUP_DIM, POOL_GROUP_DIM), lambda i, g: (j, g, 0, 0)),
            pl.BlockSpec((None, 1, POOL_GROUP_DIM), lambda i, g: (j, 0, g)),
        ],
        out_specs=pl.BlockSpec((MIX_TT, D_MODEL), lambda i, g: (i, 0)),
        out_shape=jax.ShapeDtypeStruct((t, D_MODEL), F32),
        scratch_shapes=[
            pltpu.VMEM((MIX_TT, 1), F32),
            pltpu.VMEM((MIX_TT, POOL_GROUP_DIM), BF16),
            buf, buf, buf,
        ],
        compiler_params=pltpu.CompilerParams(
            dimension_semantics=("parallel", "arbitrary"), vmem_limit_bytes=VMEM_LIMIT),
        name="pool_mix",
    )(x, mod, norm_g, pool_w, pool_scale)


def kernel(x_prompt, x_sample, state_rglru, c, c_ctx, mod_w, mod_b, norm_g, ffn_w1, ffn_w3, ffn_w2,
           rg_w_in, rg_conv_w, rg_conv_b, rg_w_a, rg_b_a, rg_w_x, rg_b_x, rg_lam, rg_w_out,
           pool_w, pool_scale):
    batch, seq, _ = x_prompt.shape
    dec_batch, dec_seq, _ = x_sample.shape
    ctx_rows, lat_rows = batch * seq, dec_batch * dec_seq
    assert seq == SEG and dec_seq == MIX_TT and dec_seq % GRID_W == 0
    assert ctx_rows % MIX_TT == 0 and MIX_TT % FFN_TM == 0 and FFN_TM == RES_TM
    assert 1 + dec_batch <= MOD_ROWS
    n_rg = rg_w_in.shape[0]

    cond = jnp.zeros((MOD_ROWS, D_MODEL), F32).at[0].set(c_ctx).at[1:1 + dec_batch].set(c)
    mod = _modulation(cond, mod_w, mod_b).reshape(DEPTH, MOD_ROWS, N_MOD, D_MODEL)

    wa, wx = _block_diag_pairs(rg_w_a), _block_diag_pairs(rg_w_x)
    w_gate = (0.5 * jnp.concatenate([wa[:, 0], wx[:, 0], wa[:, 1], wx[:, 1]], axis=-1)).astype(BF16)
    rg = (rg_w_in, rg_conv_w, rg_conv_b.reshape(n_rg, 1, D_RNN), w_gate, rg_b_a, rg_b_x, rg_lam)
    pool_scale3 = pool_scale.reshape(pool_scale.shape[0], 1, D_MODEL)
    zeros = jnp.zeros((batch, D_RNN), F32)

    x = jnp.concatenate([x_prompt.reshape(ctx_rows, D_MODEL), x_sample.reshape(lat_rows, D_MODEL)], axis=0)
    ffn_row = _mod_row_map(FFN_TM, ctx_rows, dec_seq)
    mix_row = _mod_row_map(MIX_TT, ctx_rows, dec_seq)
    n_ffn_ctx, n_ffn_lat = ctx_rows // FFN_TM, lat_rows // FFN_TM
    n_mix_ctx, n_mix_lat = ctx_rows // MIX_TT, lat_rows // MIX_TT
    ffn = functools.partial(_ffn, mod=mod, norm_g=norm_g, w1=ffn_w1, w3=ffn_w3, w2=ffn_w2, row=ffn_row)

    ctx_states = []
    for l in range(DEPTH):
        j = l // 2
        x = ffn(x, layer=l, stage=0, tile0=0, n_tiles=n_ffn_ctx + n_ffn_lat)
        if l % 2 == 0:
            scan = functools.partial(_rg_scan, x, mod, norm_g, rg=rg, layer=l, j=j, row=mix_row)
            y_ctx, sf, sb = scan(zeros, zeros, chunked=False, tile0=0, n_tiles=n_mix_ctx)
            y_lat = scan(state_rglru[:, j, 0], state_rglru[:, j, 1], chunked=True,
                         tile0=n_mix_ctx, n_tiles=n_mix_lat)
            ctx_states.append(jnp.stack([sf, sb], axis=1))
            x = _out_proj(x, y_ctx, y_lat, mod, norm_g, rg_w_out, layer=l, j=j, row=ffn_row)
        else:
            x = _pool_mix(x, mod, norm_g, pool_w, pool_scale3, layer=l, j=j, row=mix_row, n_ctx=n_mix_ctx)
        if l + 1 < DEPTH:
            x = ffn(x, layer=l, stage=2, tile0=0, n_tiles=n_ffn_ctx + n_ffn_lat)
    y_prompt = ffn(x, layer=DEPTH - 1, stage=2, tile0=0, n_tiles=n_ffn_ctx)
    y_sample = ffn(x, layer=DEPTH - 1, stage=2, tile0=n_ffn_ctx, n_tiles=n_ffn_lat)
    new_state = jnp.stack(ctx_states, axis=1)
    return (y_prompt.reshape(batch, seq, D_MODEL), y_sample.reshape(dec_batch, dec_seq, D_MODEL), new_state)
```

```python
import functools

import jax
import jax.numpy as jnp
from jax import lax
from jax.experimental import pallas as pl
from jax.experimental.pallas import tpu as pltpu

D_MODEL = 1024
DEPTH = 4
D_RNN = 1024
RNN_HEADS = 8
RNN_HEAD_DIM = D_RNN // RNN_HEADS
CONV_W = 4
RG_C = 8.0
GRID_W = 64
POOL_WINDOWS = (2, 4, 8, 16)
POOL_GROUPS = 4
POOL_GROUP_DIM = D_MODEL // POOL_GROUPS
D_FF = 2816
N_MOD = 9
EPS = 1e-6

F32 = jnp.float32
BF16 = jnp.bfloat16

MOD_ROWS = 8
MOD_TN = 1024
FFN_TM = 1024
FFN_TF = 256
FFN_TN = 256
FFN_RB = 256
FFN_NA = D_FF // FFN_TF
FFN_STAGE_SLOTS = 8
FFN_CH_UP = 32
FFN_CH_DN = 64
FFN_NB = D_MODEL // FFN_TN
RES_TM = 1024
SEG = 256
N_SEG = 8
MIX_TT = SEG * N_SEG
SEG_GAP = 4
SEG_PITCH = SEG + SEG_GAP
SCAN_ROWS = SEG_GAP + N_SEG * SEG_PITCH + 4
RG_CHUNK = 2 * SEG
HP_W = 2 * RNN_HEAD_DIM
N_HP = D_RNN // HP_W
LANES = 128
N_SLAB = HP_W // LANES
POOL_HALO = max(POOL_WINDOWS) // 2
POOL_EDGE = 8
VMEM_LIMIT = 60 * 1024 * 1024


def _dot(a, b):
    return jnp.dot(a, b, preferred_element_type=F32)


def _rms(x, g):
    ms = jnp.mean(x * x, axis=-1, keepdims=True)
    return x * lax.rsqrt(ms + EPS) * g


def _sigmoid(x):
    return 0.5 * jnp.tanh(0.5 * x) + 0.5


def _norm_modulate(x, ng_ref, mod_ref, stage):
    g = ng_ref[2 * stage:2 * stage + 1, :]
    shift = mod_ref[3 * stage:3 * stage + 1, :]
    scale = mod_ref[3 * stage + 1:3 * stage + 2, :]
    inv = lax.rsqrt(jnp.mean(x * x, axis=-1, keepdims=True) + EPS)
    return (x * inv) * (g * (1.0 + scale)) + shift


def _mod_row_map(tile_rows, ctx_rows, lat_rows):
    def row(tile):
        start = tile * tile_rows
        return jnp.where(start < ctx_rows, 0, 1 + (start - ctx_rows) // lat_rows)
    return row


def _mod_kernel(c_ref, w_ref, b_ref, o_ref):
    c = c_ref[...]
    s = c * _sigmoid(c)
    o_ref[...] = _dot(s.astype(BF16), w_ref[...].astype(BF16)) + b_ref[...]


def _modulation(cond, mod_w, mod_b):
    tn = MOD_TN
    n_col = (N_MOD * D_MODEL) // tn
    return pl.pallas_call(
        _mod_kernel,
        grid=(DEPTH, n_col),
        in_specs=[
            pl.BlockSpec((MOD_ROWS, D_MODEL), lambda l, n: (0, 0)),
            pl.BlockSpec((None, D_MODEL, tn), lambda l, n: (l, 0, n)),
            pl.BlockSpec((None, 1, tn), lambda l, n: (l, 0, n)),
        ],
        out_specs=pl.BlockSpec((None, MOD_ROWS, tn), lambda l, n: (l, 0, n)),
        out_shape=jax.ShapeDtypeStruct((DEPTH, MOD_ROWS, N_MOD * D_MODEL), F32),
        compiler_params=pltpu.CompilerParams(
            dimension_semantics=("arbitrary", "arbitrary"), vmem_limit_bytes=VMEM_LIMIT),
        name="modulation",
    )(cond, mod_w, mod_b.reshape(DEPTH, 1, N_MOD * D_MODEL))


def _weight_stream(pairs, stage_ref, sem_ref, chunk):
    jobs = [(src, dst, c * chunk) for src, dst in pairs for c in range(src.shape[0] // chunk)]

    def copy(n):
        src, _, row0 = jobs[n]
        slot = n % FFN_STAGE_SLOTS
        return pltpu.make_async_copy(src.at[pl.ds(row0, chunk), :], stage_ref.at[slot], sem_ref.at[slot])

    def prime():
        for n in range(min(FFN_STAGE_SLOTS, len(jobs))):
            copy(n).start()

    def drain():
        for n, (_, dst, row0) in enumerate(jobs):
            copy(n).wait()
            dst[row0:row0 + chunk, :] = stage_ref[n % FFN_STAGE_SLOTS].astype(BF16)
            if n + FFN_STAGE_SLOTS < len(jobs):
                copy(n + FFN_STAGE_SLOTS).start()

    return prime, drain


def _ffn_kernel(x_ref, xn_ref, mod_ref, modn_ref, ng_ref, w1_hbm, w3_hbm, w2_hbm, o_ref,
                hn_ref, p_ref, w1_ref, w3_ref, w2_ref, up_stage, dn_stage, up_sem, dn_sem,
                *, layer, sub, stage):
    @pl.when(pl.program_id(0) == 0)
    def _():
        prime_up, drain_up = _weight_stream(
            [(w1_hbm.at[layer, sub], w1_ref), (w3_hbm.at[layer, sub], w3_ref)], up_stage, up_sem, FFN_CH_UP)
        prime_dn, drain_dn = _weight_stream([(w2_hbm.at[layer, sub], w2_ref)], dn_stage, dn_sem, FFN_CH_DN)
        prime_up()
        prime_dn()
        hn_ref[...] = _norm_modulate(x_ref[...], ng_ref, mod_ref, stage).astype(BF16)
        drain_up()
        drain_dn()

    gain = ng_ref[2 * stage + 1:2 * stage + 2, :] * (0.5 * mod_ref[3 * stage + 2:3 * stage + 3, :])
    n_blocks = FFN_TM // FFN_RB
    block_rows = [slice(r * FFN_RB, (r + 1) * FFN_RB) for r in range(n_blocks)]

    def up_project(rows):
        for k in range(FFN_NA):
            cols = slice(k * FFN_TF, (k + 1) * FFN_TF)
            a = _dot(hn_ref[rows, :], w1_ref[:, cols])
            b = _dot(hn_ref[rows, :], w3_ref[:, cols])
            p_ref[k, rows, :] = (a * _sigmoid(a) * b).astype(BF16)

    def down_project(r):
        rows = block_rows[r]
        piece = FFN_RB // FFN_NB
        accs = []
        for n in range(FFN_NB):
            nrows = slice(r * FFN_RB + n * piece, r * FFN_RB + (n + 1) * piece)
            hn_ref[nrows, :] = _norm_modulate(xn_ref[nrows, :], ng_ref, modn_ref, stage).astype(BF16)
            cols = slice(n * FFN_TN, (n + 1) * FFN_TN)
            acc = _dot(p_ref[0, rows, :], w2_ref[0:FFN_TF, cols])
            for k in range(1, FFN_NA):
                acc += _dot(p_ref[k, rows, :], w2_ref[k * FFN_TF:(k + 1) * FFN_TF, cols])
            accs.append(acc)
        ss = sum(jnp.sum(acc * acc, axis=-1, keepdims=True) for acc in accs)
        inv = lax.rsqrt(ss * (1.0 / D_MODEL) + EPS)
        for n, acc in enumerate(accs):
            cols = slice(n * FFN_TN, (n + 1) * FFN_TN)
            o_ref[rows, cols] = x_ref[rows, cols] + (acc * inv) * gain[:, cols]

    up_project(block_rows[0])
    for r in range(n_blocks):
        if r + 1 < n_blocks:
            up_project(block_rows[r + 1])
        down_project(r)


def _ffn(x, mod, norm_g, w1, w3, w2, *, layer, stage, row, tile0, n_tiles):
    sub = 0 if stage == 0 else 1
    cur = lambda i: tile0 + i
    nxt = lambda i: tile0 + jnp.minimum(i + 1, n_tiles - 1)
    return pl.pallas_call(
        functools.partial(_ffn_kernel, layer=layer, sub=sub, stage=stage),
        grid=(n_tiles,),
        in_specs=[
            pl.BlockSpec((FFN_TM, D_MODEL), lambda i: (cur(i), 0)),
            pl.BlockSpec((FFN_TM, D_MODEL), lambda i: (nxt(i), 0)),
            pl.BlockSpec((None, None, N_MOD, D_MODEL), lambda i: (layer, row(cur(i)), 0, 0)),
            pl.BlockSpec((None, None, N_MOD, D_MODEL), lambda i: (layer, row(nxt(i)), 0, 0)),
            pl.BlockSpec((None, 6, D_MODEL), lambda i: (layer, 0, 0)),
            pl.BlockSpec(memory_space=pl.ANY),
            pl.BlockSpec(memory_space=pl.ANY),
            pl.BlockSpec(memory_space=pl.ANY),
        ],
        out_specs=pl.BlockSpec((FFN_TM, D_MODEL), lambda i: (i, 0)),
        out_shape=jax.ShapeDtypeStruct((n_tiles * FFN_TM, D_MODEL), F32),
        scratch_shapes=[
            pltpu.VMEM((FFN_TM, D_MODEL), BF16),
            pltpu.VMEM((FFN_NA, FFN_TM, FFN_TF), BF16),
            pltpu.VMEM((D_MODEL, D_FF), BF16),
            pltpu.VMEM((D_MODEL, D_FF), BF16),
            pltpu.VMEM((D_FF, D_MODEL), BF16),
            pltpu.VMEM((FFN_STAGE_SLOTS, FFN_CH_UP, D_FF), F32),
            pltpu.VMEM((FFN_STAGE_SLOTS, FFN_CH_DN, D_MODEL), F32),
            pltpu.SemaphoreType.DMA((FFN_STAGE_SLOTS,)),
            pltpu.SemaphoreType.DMA((FFN_STAGE_SLOTS,)),
        ],
        compiler_params=pltpu.CompilerParams(
            dimension_semantics=("arbitrary",), vmem_limit_bytes=VMEM_LIMIT),
        name="ffn",
    )(x, x, mod, mod, norm_g, w1, w3, w2)


def _out_proj_kernel(x_ref, yc_ref, yl_ref, mod_ref, ng_ref, w_ref, o_ref, *, n_ctx):
    y_in = jnp.where(pl.program_id(0) < n_ctx, yc_ref[...], yl_ref[...])
    y = _dot(y_in, w_ref[...].astype(BF16))
    o_ref[...] = x_ref[...] + mod_ref[5:6, :] * _rms(y, ng_ref[3:4, :])


def _out_proj(x, y_ctx, y_lat, mod, norm_g, w_out, *, layer, j, row):
    n_ctx = y_ctx.shape[0] // RES_TM
    n_lat = y_lat.shape[0] // RES_TM
    return pl.pallas_call(
        functools.partial(_out_proj_kernel, n_ctx=n_ctx),
        grid=(n_ctx + n_lat,),
        in_specs=[
            pl.BlockSpec((RES_TM, D_MODEL), lambda i: (i, 0)),
            pl.BlockSpec((RES_TM, D_RNN), lambda i: (jnp.minimum(i, n_ctx - 1), 0)),
            pl.BlockSpec((RES_TM, D_RNN), lambda i: (jnp.maximum(i - n_ctx, 0), 0)),
            pl.BlockSpec((None, None, N_MOD, D_MODEL), lambda i: (layer, row(i), 0, 0)),
            pl.BlockSpec((None, 6, D_MODEL), lambda i: (layer, 0, 0)),
            pl.BlockSpec((None, D_RNN, D_MODEL), lambda i: (j, 0, 0)),
        ],
        out_specs=pl.BlockSpec((RES_TM, D_MODEL), lambda i: (i, 0)),
        out_shape=jax.ShapeDtypeStruct(x.shape, F32),
        compiler_params=pltpu.CompilerParams(
            dimension_semantics=("parallel",), vmem_limit_bytes=VMEM_LIMIT),
        name="out_proj",
    )(x, y_ctx, y_lat, mod, norm_g, w_out)


def _seg_base(k):
    return SEG_GAP + k * SEG_PITCH


def _seg_rows(k, offset=0):
    return pl.ds(_seg_base(k) + offset, SEG)


def _rg_kernel(x_ref, mod_ref, ng_ref, h0f_ref, h0b_ref, wig_ref, wix_ref, cw_ref, cb_ref, wg_ref,
               ba_ref, bx_ref, lam_ref, *rest, chunked):
    if chunked:
        y_ref, hn_ref, gg_ref, xr_ref, a_ref, u_ref, h_ref, p_ref = rest
    else:
        y_ref, sf_ref, sb_ref, hn_ref, gg_ref, xr_ref, a_ref, u_ref, h_ref = rest
    hp = pl.program_id(1)

    @pl.when(hp == 0)
    def _():
        hn_ref[...] = _norm_modulate(x_ref[...], ng_ref, mod_ref, 1).astype(BF16)

    wig = wig_ref[...].astype(BF16)
    wix = wix_ref[...].astype(BF16)
    seg_per_chunk = RG_CHUNK // SEG
    zero2 = jnp.zeros((2, LANES), F32)

    def project(c):
        rows = slice(c * RG_CHUNK, (c + 1) * RG_CHUNK)
        hn = hn_ref[rows, :]
        gg_ref[rows, :] = jax.nn.gelu(_dot(hn, wig))
        xr = _dot(hn, wix)
        for s in range(N_SLAB):
            lanes = slice(s * LANES, (s + 1) * LANES)
            for kk in range(seg_per_chunk):
                k = c * seg_per_chunk + kk
                base = _seg_base(k)
                seg = xr[kk * SEG:(kk + 1) * SEG, lanes]
                xr_ref[s, _seg_rows(k), :] = seg
                inside = chunked and k > 0
                xr_ref[s, base - SEG_GAP:base - 2, :] = seg[0:2, :] if inside else zero2
                inside = chunked and k < N_SEG - 1
                xr_ref[s, base + SEG + 2:base + SEG + SEG_GAP, :] = seg[SEG - 2:SEG, :] if inside else zero2
                if k == 0:
                    xr_ref[s, base - 2:base, :] = zero2
                if k == N_SEG - 1:
                    xr_ref[s, base + SEG:base + SEG + 2, :] = zero2

    z = -lam_ref[...]
    softplus = jnp.maximum(z, 0.0) + jnp.log1p(jnp.exp(-jnp.abs(z)))
    half_rate = (0.5 * (-RG_C)) * softplus
    half_ba = 0.5 * ba_ref[...]
    half_bx = 0.5 * bx_ref[...]
    cw = cw_ref[...]
    cb = cb_ref[...]

    def gates(k):
        taps = []
        for s in range(N_SLAB):
            lanes = slice(s * LANES, (s + 1) * LANES)
            acc = cb[:, lanes] + cw[0:1, lanes] * xr_ref[s, _seg_rows(k, -1), :]
            for j in range(1, CONV_W):
                acc += cw[j:j + 1, lanes] * xr_ref[s, _seg_rows(k, j - 1), :]
            taps.append(acc)
        xk = jnp.concatenate(taps, axis=1)
        pre = _dot(xk.astype(BF16), wg_ref[...])
        xh = 0.5 * xk
        for d in range(2):
            tr = jnp.tanh(pre[:, (2 * d) * HP_W:(2 * d + 1) * HP_W] + half_ba[d:d + 1, :])
            ti = jnp.tanh(pre[:, (2 * d + 1) * HP_W:(2 * d + 2) * HP_W] + half_bx[d:d + 1, :])
            a = jnp.exp(tr * half_rate[d:d + 1, :] + half_rate[d:d + 1, :])
            v = 1.0 - a * a
            beta = jnp.where(v > 0.0, v * lax.rsqrt(v), 0.0)
            u = beta * (xh * ti + xh)
            for s in range(N_SLAB):
                lanes = slice(s * LANES, (s + 1) * LANES)
                a_ref[d, s, _seg_rows(k), :] = a[:, lanes]
                u_ref[d, s, _seg_rows(k), :] = u[:, lanes]

    n_chunk = MIX_TT // RG_CHUNK
    project(0)
    for c in range(n_chunk):
        if c + 1 < n_chunk:
            project(c + 1)
        first = c * seg_per_chunk - 1
        for k in range(max(first, 0), first + seg_per_chunk):
            gates(k)
    gates(N_SEG - 1)

    def step(t, carry):
        hs, ps = carry
        new_h, new_p = [], []
        for d in range(2):
            row = SEG_GAP + (t if d == 0 else SEG - 1 - t)
            idx = pl.ds(row, N_SEG, stride=SEG_PITCH)
            for s in range(N_SLAB):
                a = a_ref[d, s, idx, :]
                h = a * hs[d * N_SLAB + s] + u_ref[d, s, idx, :]
                h_ref[d, s, idx, :] = h
                new_h.append(h)
                if chunked:
                    p = a * ps[d * N_SLAB + s]
                    p_ref[d, s, idx, :] = p
                    new_p.append(p)
        return tuple(new_h), tuple(new_p)

    if chunked:
        h_init = tuple(jnp.zeros((N_SEG, LANES), F32) for _ in range(2 * N_SLAB))
        p_init = tuple(jnp.ones((N_SEG, LANES), F32) for _ in range(2 * N_SLAB))
    else:
        h0 = (h0f_ref[...], h0b_ref[...])
        h_init = tuple(h0[d][:, s * LANES:(s + 1) * LANES] for d in range(2) for s in range(N_SLAB))
        p_init = ()
    h_fin, _ = lax.fori_loop(0, SEG, step, (h_init, p_init), unroll=8)

    if chunked:
        b = pl.program_id(0)
        h0 = (h0f_ref[pl.ds(b, 1), :], h0b_ref[pl.ds(b, 1), :])
        for s in range(N_SLAB):
            lanes = slice(s * LANES, (s + 1) * LANES)
            carry = h0[0][:, lanes]
            for k in range(N_SEG):
                h = h_ref[0, s, _seg_rows(k), :] + p_ref[0, s, _seg_rows(k), :] * carry
                h_ref[0, s, _seg_rows(k), :] = h
                carry = h[SEG - 1:SEG, :]
            carry = h0[1][:, lanes]
            for k in reversed(range(N_SEG)):
                hb = h_ref[1, s, _seg_rows(k), :] + p_ref[1, s, _seg_rows(k), :] * carry
                carry = hb[0:1, :]
                rows = slice(k * SEG, (k + 1) * SEG)
                y = (h_ref[0, s, _seg_rows(k), :] + hb) * gg_ref[rows, lanes]
                y_ref[rows, lanes] = y.astype(BF16)
    else:
        for s in range(N_SLAB):
            lanes = slice(s * LANES, (s + 1) * LANES)
            sf_ref[:, lanes] = h_fin[s]
            sb_ref[:, lanes] = h_fin[N_SLAB + s]
            for k in range(N_SEG):
                rows = slice(k * SEG, (k + 1) * SEG)
                y = (h_ref[0, s, _seg_rows(k), :] + h_ref[1, s, _seg_rows(k), :]) * gg_ref[rows, lanes]
                y_ref[rows, lanes] = y.astype(BF16)


def _rg_scan(x, mod, norm_g, h0f, h0b, rg, *, layer, j, chunked, row, tile0, n_tiles):
    t = n_tiles * MIX_TT
    w_in, conv_w, conv_b, w_gate, b_a, b_x, lam = rg
    n_state = h0f.shape[0]
    state_block = (n_state, HP_W) if chunked else (N_SEG, HP_W)
    state_map = (lambda i, hp: (0, hp)) if chunked else (lambda i, hp: (i, hp))
    in_specs = [
        pl.BlockSpec((MIX_TT, D_MODEL), lambda i, hp: (tile0 + i, 0)),
        pl.BlockSpec((None, None, N_MOD, D_MODEL), lambda i, hp: (layer, row(tile0 + i), 0, 0)),
        pl.BlockSpec((None, 6, D_MODEL), lambda i, hp: (layer, 0, 0)),
        pl.BlockSpec(state_block, state_map),
        pl.BlockSpec(state_block, state_map),
        pl.BlockSpec((None, D_MODEL, HP_W), lambda i, hp: (j, 0, hp)),
        pl.BlockSpec((None, D_MODEL, HP_W), lambda i, hp: (j, 0, N_HP + hp)),
        pl.BlockSpec((None, CONV_W, HP_W), lambda i, hp: (j, 0, hp)),
        pl.BlockSpec((None, 1, HP_W), lambda i, hp: (j, 0, hp)),
        pl.BlockSpec((None, None, HP_W, 4 * HP_W), lambda i, hp: (j, hp, 0, 0)),
        pl.BlockSpec((None, 2, HP_W), lambda i, hp: (j, 0, hp)),
        pl.BlockSpec((None, 2, HP_W), lambda i, hp: (j, 0, hp)),
        pl.BlockSpec((None, 2, HP_W), lambda i, hp: (j, 0, hp)),
    ]
    y_spec = pl.BlockSpec((MIX_TT, HP_W), lambda i, hp: (i, hp))
    y_shape = jax.ShapeDtypeStruct((t, D_RNN), BF16)
    if chunked:
        out_specs, out_shape = y_spec, y_shape
    else:
        s_spec = pl.BlockSpec((N_SEG, HP_W), lambda i, hp: (i, hp))
        s_shape = jax.ShapeDtypeStruct((t // SEG, D_RNN), F32)
        out_specs, out_shape = (y_spec, s_spec, s_spec), (y_shape, s_shape, s_shape)
    scan = pltpu.VMEM((2, N_SLAB, SCAN_ROWS, LANES), F32)
    scratch = [
        pltpu.VMEM((MIX_TT, D_MODEL), BF16),
        pltpu.VMEM((MIX_TT, HP_W), F32),
        pltpu.VMEM((N_SLAB, SCAN_ROWS, LANES), F32),
        scan, scan, scan,
    ] + ([scan] if chunked else [])
    return pl.pallas_call(
        functools.partial(_rg_kernel, chunked=chunked),
        grid=(t // MIX_TT, N_HP),
        in_specs=in_specs,
        out_specs=out_specs,
        out_shape=out_shape,
        scratch_shapes=scratch,
        compiler_params=pltpu.CompilerParams(
            dimension_semantics=("parallel", "arbitrary"), vmem_limit_bytes=VMEM_LIMIT),
        name="rg_scan",
    )(x, mod, norm_g, h0f, h0b, w_in, w_in, conv_w, conv_b, w_gate, b_a, b_x, lam)


def _block_diag_pairs(w):
    n = w.shape[0]
    w = w.reshape(n, 2, N_HP, 2, RNN_HEAD_DIM, RNN_HEAD_DIM)
    z = jnp.zeros_like(w[:, :, :, 0])
    top = jnp.concatenate([w[:, :, :, 0], z], axis=-1)
    bot = jnp.concatenate([z, w[:, :, :, 1]], axis=-1)
    return jnp.concatenate([top, bot], axis=-2)


def _pool_layout(on_grid):
    run, n_run, pad_runs = (GRID_W, MIX_TT // GRID_W, POOL_HALO) if on_grid else (SEG, N_SEG, 0)
    pitch = run + POOL_HALO
    rows = 2 * POOL_EDGE + POOL_HALO + (n_run + 2 * pad_runs) * pitch
    return run, n_run, pad_runs, pitch, rows


def _window_count(half, pos, length):
    return (jnp.minimum(pos + half, length) - jnp.maximum(pos - half, 0)).astype(F32)


def _pool_kernel(x_ref, mod_ref, ng_ref, pw_ref, ps_ref, o_ref, inv_ref, d_ref, xb_ref, b0_ref, b1_ref,
                 *, n_ctx):
    i = pl.program_id(0)
    g = pl.program_id(1)

    @pl.when(g == 0)
    def _():
        x = x_ref[...]
        inv_ref[...] = lax.rsqrt(jnp.mean(x * x, axis=-1, keepdims=True) + EPS)

    row_id = lax.broadcasted_iota(jnp.int32, (MIX_TT, 1), 0)
    zero_edge = jnp.zeros((POOL_EDGE, LANES), F32)

    def pool_group(gi, w, on_grid):
        run, n_run, pad_runs, pitch, n_rows = _pool_layout(on_grid)
        base = lambda r: POOL_EDGE + POOL_HALO + (r + pad_runs) * pitch
        levels = w.bit_length() - 1
        half = w // 2

        def window_sums(s):
            src, bufs = xb_ref, [b0_ref, b1_ref]
            lo, hi = POOL_EDGE, n_rows - POOL_EDGE
            for lvl in range(levels):
                dst = bufs[lvl % 2]
                dn, up = (1, 0) if lvl == 0 else (2 ** (lvl - 1),) * 2
                dst[s, lo:hi, :] = src[s, lo - dn:hi - dn, :] + src[s, lo + up:hi + up, :]
                src = dst
            if on_grid:
                reach = 0
                for lvl in range(levels):
                    dst = bufs[(levels + lvl) % 2]
                    dn, up = (1, 0) if lvl == 0 else (2 ** (lvl - 1),) * 2
                    reach += dn
                    lo, hi = base(reach - pad_runs), base(n_run + pad_runs - reach)
                    dst[s, lo:hi, :] = (src[s, lo - dn * pitch:hi - dn * pitch, :]
                                        + src[s, lo + up * pitch:hi + up * pitch, :])
                    src = dst
            return src

        cols = slice(gi * POOL_GROUP_DIM, (gi + 1) * POOL_GROUP_DIM)
        scale = ng_ref[2:3, cols] * (1.0 + mod_ref[4:5, cols])
        xg = (x_ref[:, cols] * inv_ref[...]) * scale + mod_ref[3:4, cols]
        if on_grid:
            cnt = (_window_count(half, row_id % GRID_W, GRID_W)
                   * _window_count(half, row_id // GRID_W, n_run))
        else:
            cnt = _window_count(half, row_id % SEG, SEG)
        inv_cnt = 1.0 / cnt
        for s in range(POOL_GROUP_DIM // LANES):
            lanes = slice(s * LANES, (s + 1) * LANES)
            xb_ref[s, 0:n_rows, :] = jnp.zeros((n_rows, LANES), F32)
            for r in range(n_run):
                xb_ref[s, base(r):base(r) + run, :] = xg[r * run:(r + 1) * run, lanes]
            for buf in (b0_ref, b1_ref):
                buf[s, 0:POOL_EDGE, :] = zero_edge
                buf[s, n_rows - POOL_EDGE:n_rows, :] = zero_edge
            tot = window_sums(s)
            for r in range(n_run):
                rows = slice(r * run, (r + 1) * run)
                mean = tot[s, base(r):base(r) + run, :] * inv_cnt[rows, :]
                d_ref[rows, lanes] = (mean - xb_ref[s, base(r):base(r) + run, :]).astype(BF16)
        o_ref[:, cols] = _dot(d_ref[...], pw_ref[...].astype(BF16)) * ps_ref[...]

    for on_grid in (False, True):
        for gi, w in enumerate(POOL_WINDOWS):
            pl.when((g == gi) & ((i >= n_ctx) == on_grid))(functools.partial(pool_group, gi, w, on_grid))

    @pl.when(g == POOL_GROUPS - 1)
    def _():
        gain = mod_ref[5:6, :] * ng_ref[3:4, :]
        for r in range(MIX_TT // SEG):
            rows = slice(r * SEG, (r + 1) * SEG)
            y = o_ref[rows, :]
            inv = lax.rsqrt(jnp.mean(y * y, axis=-1, keepdims=True) + EPS)
            o_ref[rows, :] = x_ref[rows, :] + (y * inv) * gain


def _pool_mix(x, mod, norm_g, pool_w, pool_scale, *, layer, j, row, n_ctx):
    t = x.shape[0]
    n_rows = max(_pool_layout(False)[-1], _pool_layout(True)[-1])
    buf = pltpu.VMEM((POOL_GROUP_DIM // LANES, n_rows, LANES), F32)
    return pl.pallas_call(
        functools.partial(_pool_kernel, n_ctx=n_ctx),
        grid=(t // MIX_TT, POOL_GROUPS),
        in_specs=[
            pl.BlockSpec((MIX_TT, D_MODEL), lambda i, g: (i, 0)),
            pl.BlockSpec((None, None, N_MOD, D_MODEL), lambda i, g: (layer, row(i), 0, 0)),
            pl.BlockSpec((None, 6, D_MODEL), lambda i, g: (layer, 0, 0)),
            pl.BlockSpec((None, None, POOL_GROUP_DIM, POOL_GROUP_DIM), lambda i, g: (j, g, 0, 0)),
            pl.BlockSpec((None, 1, POOL_GROUP_DIM), lambda i, g: (j, 0, g)),
        ],
        out_specs=pl.BlockSpec((MIX_TT, D_MODEL), lambda i, g: (i, 0)),
        out_shape=jax.ShapeDtypeStruct((t, D_MODEL), F32),
        scratch_shapes=[
            pltpu.VMEM((MIX_TT, 1), F32),
            pltpu.VMEM((MIX_TT, POOL_GROUP_DIM), BF16),
            buf, buf, buf,
        ],
        compiler_params=pltpu.CompilerParams(
            dimension_semantics=("parallel", "arbitrary"), vmem_limit_bytes=VMEM_LIMIT),
        name="pool_mix",
    )(x, mod, norm_g, pool_w, pool_scale)


def kernel(x_prompt, x_sample, state_rglru, c, c_ctx, mod_w, mod_b, norm_g, ffn_w1, ffn_w3, ffn_w2,
           rg_w_in, rg_conv_w, rg_conv_b, rg_w_a, rg_b_a, rg_w_x, rg_b_x, rg_lam, rg_w_out,
           pool_w, pool_scale):
    batch, seq, _ = x_prompt.shape
    dec_batch, dec_seq, _ = x_sample.shape
    ctx_rows, lat_rows = batch * seq, dec_batch * dec_seq
    assert seq == SEG and dec_seq == MIX_TT and dec_seq % GRID_W == 0
    assert ctx_rows % MIX_TT == 0 and MIX_TT % FFN_TM == 0 and FFN_TM == RES_TM
    assert 1 + dec_batch <= MOD_ROWS
    n_rg = rg_w_in.shape[0]

    cond = jnp.zeros((MOD_ROWS, D_MODEL), F32).at[0].set(c_ctx).at[1:1 + dec_batch].set(c)
    mod = _modulation(cond, mod_w, mod_b).reshape(DEPTH, MOD_ROWS, N_MOD, D_MODEL)

    wa, wx = _block_diag_pairs(rg_w_a), _block_diag_pairs(rg_w_x)
    w_gate = (0.5 * jnp.concatenate([wa[:, 0], wx[:, 0], wa[:, 1], wx[:, 1]], axis=-1)).astype(BF16)
    rg = (rg_w_in, rg_conv_w, rg_conv_b.reshape(n_rg, 1, D_RNN), w_gate, rg_b_a, rg_b_x, rg_lam)
    pool_scale3 = pool_scale.reshape(pool_scale.shape[0], 1, D_MODEL)
    zeros = jnp.zeros((batch, D_RNN), F32)

    x = jnp.concatenate([x_prompt.reshape(ctx_rows, D_MODEL), x_sample.reshape(lat_rows, D_MODEL)], axis=0)
    ffn_row = _mod_row_map(FFN_TM, ctx_rows, dec_seq)
    mix_row = _mod_row_map(MIX_TT, ctx_rows, dec_seq)
    n_ffn_ctx, n_ffn_lat = ctx_rows // FFN_TM, lat_rows // FFN_TM
    n_mix_ctx, n_mix_lat = ctx_rows // MIX_TT, lat_rows // MIX_TT
    ffn = functools.partial(_ffn, mod=mod, norm_g=norm_g, w1=ffn_w1, w3=ffn_w3, w2=ffn_w2, row=ffn_row)

    ctx_states = []
    for l in range(DEPTH):
        j = l // 2
        x = ffn(x, layer=l, stage=0, tile0=0, n_tiles=n_ffn_ctx + n_ffn_lat)
        if l % 2 == 0:
            scan = functools.partial(_rg_scan, x, mod, norm_g, rg=rg, layer=l, j=j, row=mix_row)
            y_ctx, sf, sb = scan(zeros, zeros, chunked=False, tile0=0, n_tiles=n_mix_ctx)
            y_lat = scan(state_rglru[:, j, 0], state_rglru[:, j, 1], chunked=True,
                         tile0=n_mix_ctx, n_tiles=n_mix_lat)
            ctx_states.append(jnp.stack([sf, sb], axis=1))
            x = _out_proj(x, y_ctx, y_lat, mod, norm_g, rg_w_out, layer=l, j=j, row=ffn_row)
        else:
            x = _pool_mix(x, mod, norm_g, pool_w, pool_scale3, layer=l, j=j, row=mix_row, n_ctx=n_mix_ctx)
        if l + 1 < DEPTH:
            x = ffn(x, layer=l, stage=2, tile0=0, n_tiles=n_ffn_ctx + n_ffn_lat)
    y_prompt = ffn(x, layer=DEPTH - 1, stage=2, tile0=0, n_tiles=n_ffn_ctx)
    y_sample = ffn(x, layer=DEPTH - 1, stage=2, tile0=n_ffn_ctx, n_tiles=n_ffn_lat)
    new_state = jnp.stack(ctx_states, axis=1)
    return (y_prompt.reshape(batch, seq, D_MODEL), y_sample.reshape(dec_batch, dec_seq, D_MODEL), new_state)
```

```python
import functools

import jax
import jax.numpy as jnp
from jax import lax
from jax.experimental import pallas as pl
from jax.experimental.pallas import tpu as pltpu

D_MODEL = 1024
DEPTH = 4
D_RNN = 1024
RNN_HEADS = 8
RNN_HEAD_DIM = D_RNN // RNN_HEADS
CONV_W = 4
RG_C = 8.0
GRID_W = 64
POOL_WINDOWS = (2, 4, 8, 16)
POOL_GROUPS = 4
POOL_GROUP_DIM = D_MODEL // POOL_GROUPS
D_FF = 2816
N_MOD = 9
EPS = 1e-6

F32 = jnp.float32
BF16 = jnp.bfloat16

MOD_ROWS = 8
MOD_TN = 1024
FFN_TM = 1024
FFN_TF = 256
FFN_TN = 256
FFN_RB = 256
FFN_NA = D_FF // FFN_TF
FFN_NB = D_MODEL // FFN_TN
RES_TM = 1024
SEG = 256
N_SEG = 8
MIX_TT = SEG * N_SEG
SEG_GAP = 8
SEG_PITCH = SEG + SEG_GAP
SCAN_ROWS = SEG_GAP + N_SEG * SEG_PITCH
RG_CHUNK = 2 * SEG
HP_W = 2 * RNN_HEAD_DIM
N_HP = D_RNN // HP_W
LANES = 128
N_SLAB = HP_W // LANES
POOL_HALO = max(POOL_WINDOWS) // 2
POOL_EDGE = 8
VMEM_LIMIT = 60 * 1024 * 1024


def _dot(a, b):
    return jnp.dot(a, b, preferred_element_type=F32)


def _rms(x, g):
    ms = jnp.mean(x * x, axis=-1, keepdims=True)
    return x * lax.rsqrt(ms + EPS) * g


def _sigmoid(x):
    return 0.5 * jnp.tanh(0.5 * x) + 0.5


def _norm_modulate(x, ng_ref, mod_ref, stage):
    g = ng_ref[2 * stage:2 * stage + 1, :]
    shift = mod_ref[3 * stage:3 * stage + 1, :]
    scale = mod_ref[3 * stage + 1:3 * stage + 2, :]
    inv = lax.rsqrt(jnp.mean(x * x, axis=-1, keepdims=True) + EPS)
    return (x * inv) * (g * (1.0 + scale)) + shift


def _mod_row_map(tile_rows, ctx_rows, lat_rows):
    def row(tile):
        start = tile * tile_rows
        return jnp.where(start < ctx_rows, 0, 1 + (start - ctx_rows) // lat_rows)
    return row


def _mod_kernel(c_ref, w_ref, b_ref, o_ref):
    c = c_ref[...]
    s = c * _sigmoid(c)
    o_ref[...] = _dot(s.astype(BF16), w_ref[...].astype(BF16)) + b_ref[...]


def _modulation(cond, mod_w, mod_b):
    tn = MOD_TN
    n_col = (N_MOD * D_MODEL) // tn
    return pl.pallas_call(
        _mod_kernel,
        grid=(DEPTH, n_col),
        in_specs=[
            pl.BlockSpec((MOD_ROWS, D_MODEL), lambda l, n: (0, 0)),
            pl.BlockSpec((None, D_MODEL, tn), lambda l, n: (l, 0, n)),
            pl.BlockSpec((None, 1, tn), lambda l, n: (l, 0, n)),
        ],
        out_specs=pl.BlockSpec((None, MOD_ROWS, tn), lambda l, n: (l, 0, n)),
        out_shape=jax.ShapeDtypeStruct((DEPTH, MOD_ROWS, N_MOD * D_MODEL), F32),
        compiler_params=pltpu.CompilerParams(
            dimension_semantics=("arbitrary", "arbitrary"), vmem_limit_bytes=VMEM_LIMIT),
        name="modulation",
    )(cond, mod_w, mod_b.reshape(DEPTH, 1, N_MOD * D_MODEL))


def _ffn_kernel(x_ref, xn_ref, mod_ref, modn_ref, ng_ref, w1_ref, w3_ref, w2_ref, o_ref, hn_ref, p_ref,
                *, stage):
    @pl.when(pl.program_id(0) == 0)
    def _():
        hn_ref[...] = _norm_modulate(x_ref[...], ng_ref, mod_ref, stage).astype(BF16)

    gain = ng_ref[2 * stage + 1:2 * stage + 2, :] * (0.5 * mod_ref[3 * stage + 2:3 * stage + 3, :])
    n_blocks = FFN_TM // FFN_RB
    block_rows = [slice(r * FFN_RB, (r + 1) * FFN_RB) for r in range(n_blocks)]

    def up_project(rows):
        for k in range(FFN_NA):
            cols = slice(k * FFN_TF, (k + 1) * FFN_TF)
            a = _dot(hn_ref[rows, :], w1_ref[:, cols])
            b = _dot(hn_ref[rows, :], w3_ref[:, cols])
            p_ref[k, rows, :] = (a * _sigmoid(a) * b).astype(BF16)

    def down_project(r):
        rows = block_rows[r]
        piece = FFN_RB // FFN_NB
        accs = []
        for n in range(FFN_NB):
            nrows = slice(r * FFN_RB + n * piece, r * FFN_RB + (n + 1) * piece)
            hn_ref[nrows, :] = _norm_modulate(xn_ref[nrows, :], ng_ref, modn_ref, stage).astype(BF16)
            cols = slice(n * FFN_TN, (n + 1) * FFN_TN)
            acc = _dot(p_ref[0, rows, :], w2_ref[0:FFN_TF, cols])
            for k in range(1, FFN_NA):
                acc += _dot(p_ref[k, rows, :], w2_ref[k * FFN_TF:(k + 1) * FFN_TF, cols])
            accs.append(acc)
        ss = sum(jnp.sum(acc * acc, axis=-1, keepdims=True) for acc in accs)
        inv = lax.rsqrt(ss * (1.0 / D_MODEL) + EPS)
        for n, acc in enumerate(accs):
            cols = slice(n * FFN_TN, (n + 1) * FFN_TN)
            o_ref[rows, cols] = x_ref[rows, cols] + (acc * inv) * gain[:, cols]

    up_project(block_rows[0])
    for r in range(n_blocks):
        if r + 1 < n_blocks:
            up_project(block_rows[r + 1])
        down_project(r)


def _ffn(x, mod, norm_g, w1, w3, w2, *, layer, stage, row, tile0, n_tiles):
    sub = 0 if stage == 0 else 1
    cur = lambda i: tile0 + i
    nxt = lambda i: tile0 + jnp.minimum(i + 1, n_tiles - 1)
    resident = dict(pipeline_mode=pl.Buffered(1))
    return pl.pallas_call(
        functools.partial(_ffn_kernel, stage=stage),
        grid=(n_tiles,),
        in_specs=[
            pl.BlockSpec((FFN_TM, D_MODEL), lambda i: (cur(i), 0)),
            pl.BlockSpec((FFN_TM, D_MODEL), lambda i: (nxt(i), 0)),
            pl.BlockSpec((None, None, N_MOD, D_MODEL), lambda i: (layer, row(cur(i)), 0, 0)),
            pl.BlockSpec((None, None, N_MOD, D_MODEL), lambda i: (layer, row(nxt(i)), 0, 0)),
            pl.BlockSpec((None, 6, D_MODEL), lambda i: (layer, 0, 0)),
            pl.BlockSpec((None, None, D_MODEL, D_FF), lambda i: (layer, sub, 0, 0), **resident),
            pl.BlockSpec((None, None, D_MODEL, D_FF), lambda i: (layer, sub, 0, 0), **resident),
            pl.BlockSpec((None, None, D_FF, D_MODEL), lambda i: (layer, sub, 0, 0), **resident),
        ],
        out_specs=pl.BlockSpec((FFN_TM, D_MODEL), lambda i: (i, 0)),
        out_shape=jax.ShapeDtypeStruct((n_tiles * FFN_TM, D_MODEL), F32),
        scratch_shapes=[
            pltpu.VMEM((FFN_TM, D_MODEL), BF16),
            pltpu.VMEM((FFN_NA, FFN_TM, FFN_TF), BF16),
        ],
        compiler_params=pltpu.CompilerParams(
            dimension_semantics=("arbitrary",), vmem_limit_bytes=VMEM_LIMIT),
        name="ffn",
    )(x, x, mod, mod, norm_g, w1, w3, w2)


def _out_proj_kernel(x_ref, yc_ref, yl_ref, mod_ref, ng_ref, w_ref, o_ref, *, n_ctx):
    y_in = jnp.where(pl.program_id(0) < n_ctx, yc_ref[...], yl_ref[...])
    y = _dot(y_in, w_ref[...].astype(BF16))
    o_ref[...] = x_ref[...] + mod_ref[5:6, :] * _rms(y, ng_ref[3:4, :])


def _out_proj(x, y_ctx, y_lat, mod, norm_g, w_out, *, layer, j, row):
    n_ctx = y_ctx.shape[0] // RES_TM
    n_lat = y_lat.shape[0] // RES_TM
    return pl.pallas_call(
        functools.partial(_out_proj_kernel, n_ctx=n_ctx),
        grid=(n_ctx + n_lat,),
        in_specs=[
            pl.BlockSpec((RES_TM, D_MODEL), lambda i: (i, 0)),
            pl.BlockSpec((RES_TM, D_RNN), lambda i: (jnp.minimum(i, n_ctx - 1), 0)),
            pl.BlockSpec((RES_TM, D_RNN), lambda i: (jnp.maximum(i - n_ctx, 0), 0)),
            pl.BlockSpec((None, None, N_MOD, D_MODEL), lambda i: (layer, row(i), 0, 0)),
            pl.BlockSpec((None, 6, D_MODEL), lambda i: (layer, 0, 0)),
            pl.BlockSpec((None, D_RNN, D_MODEL), lambda i: (j, 0, 0)),
        ],
        out_specs=pl.BlockSpec((RES_TM, D_MODEL), lambda i: (i, 0)),
        out_shape=jax.ShapeDtypeStruct(x.shape, F32),
        compiler_params=pltpu.CompilerParams(
            dimension_semantics=("parallel",), vmem_limit_bytes=VMEM_LIMIT),
        name="out_proj",
    )(x, y_ctx, y_lat, mod, norm_g, w_out)


def _seg_base(k):
    return SEG_GAP + k * SEG_PITCH


def _seg_rows(k, offset=0):
    return pl.ds(_seg_base(k) + offset, SEG)


def _rg_kernel(x_ref, mod_ref, ng_ref, h0f_ref, h0b_ref, wig_ref, wix_ref, cw_ref, cb_ref, wg_ref,
               ba_ref, bx_ref, lam_ref, *rest, chunked):
    if chunked:
        y_ref, hn_ref, gg_ref, xr_ref, a_ref, u_ref, h_ref, p_ref = rest
    else:
        y_ref, sf_ref, sb_ref, hn_ref, gg_ref, xr_ref, a_ref, u_ref, h_ref = rest
    hp = pl.program_id(1)

    @pl.when(hp == 0)
    def _():
        hn_ref[...] = _norm_modulate(x_ref[...], ng_ref, mod_ref, 1).astype(BF16)

    wig = wig_ref[...].astype(BF16)
    wix = wix_ref[...].astype(BF16)
    seg_per_chunk = RG_CHUNK // SEG
    zero2 = jnp.zeros((2, LANES), F32)

    def project(c):
        rows = slice(c * RG_CHUNK, (c + 1) * RG_CHUNK)
        hn = hn_ref[rows, :]
        gg_ref[rows, :] = jax.nn.gelu(_dot(hn, wig))
        xr = _dot(hn, wix)
        for s in range(N_SLAB):
            lanes = slice(s * LANES, (s + 1) * LANES)
            for kk in range(seg_per_chunk):
                k = c * seg_per_chunk + kk
                base = _seg_base(k)
                seg = xr[kk * SEG:(kk + 1) * SEG, lanes]
                xr_ref[s, _seg_rows(k), :] = seg
                inside = chunked and k > 0
                xr_ref[s, base - SEG_GAP:base - SEG_GAP + 2, :] = seg[0:2, :] if inside else zero2
                inside = chunked and k < N_SEG - 1
                xr_ref[s, base + SEG_PITCH - 2:base + SEG_PITCH, :] = seg[SEG - 2:SEG, :] if inside else zero2
                if k == 0:
                    xr_ref[s, base - 2:base, :] = zero2
                if k == N_SEG - 1:
                    xr_ref[s, base + SEG:base + SEG + 2, :] = zero2

    z = -lam_ref[...]
    softplus = jnp.maximum(z, 0.0) + jnp.log1p(jnp.exp(-jnp.abs(z)))
    half_rate = (0.5 * (-RG_C)) * softplus
    half_ba = 0.5 * ba_ref[...]
    half_bx = 0.5 * bx_ref[...]
    cw = cw_ref[...]
    cb = cb_ref[...]

    def gates(k):
        taps = []
        for s in range(N_SLAB):
            lanes = slice(s * LANES, (s + 1) * LANES)
            acc = cb[:, lanes] + cw[0:1, lanes] * xr_ref[s, _seg_rows(k, -1), :]
            for j in range(1, CONV_W):
                acc += cw[j:j + 1, lanes] * xr_ref[s, _seg_rows(k, j - 1), :]
            taps.append(acc)
        xk = jnp.concatenate(taps, axis=1)
        pre = _dot(xk.astype(BF16), wg_ref[...])
        xh = 0.5 * xk
        for d in range(2):
            tr = jnp.tanh(pre[:, (2 * d) * HP_W:(2 * d + 1) * HP_W] + half_ba[d:d + 1, :])
            ti = jnp.tanh(pre[:, (2 * d + 1) * HP_W:(2 * d + 2) * HP_W] + half_bx[d:d + 1, :])
            a = jnp.exp(tr * half_rate[d:d + 1, :] + half_rate[d:d + 1, :])
            v = 1.0 - a * a
            beta = jnp.where(v > 0.0, v * lax.rsqrt(v), 0.0)
            u = beta * (xh * ti + xh)
            for s in range(N_SLAB):
                lanes = slice(s * LANES, (s + 1) * LANES)
                a_ref[d, s, _seg_rows(k), :] = a[:, lanes]
                u_ref[d, s, _seg_rows(k), :] = u[:, lanes]

    n_chunk = MIX_TT // RG_CHUNK
    project(0)
    for c in range(n_chunk):
        if c + 1 < n_chunk:
            project(c + 1)
        first = c * seg_per_chunk - 1
        for k in range(max(first, 0), first + seg_per_chunk):
            gates(k)
    gates(N_SEG - 1)

    def step(t, carry):
        hs, ps = carry
        new_h, new_p = [], []
        for d in range(2):
            row = SEG_GAP + (t if d == 0 else SEG - 1 - t)
            idx = pl.ds(row, N_SEG, stride=SEG_PITCH)
            for s in range(N_SLAB):
                a = a_ref[d, s, idx, :]
                h = a * hs[d * N_SLAB + s] + u_ref[d, s, idx, :]
                h_ref[d, s, idx, :] = h
                new_h.append(h)
                if chunked:
                    p = a * ps[d * N_SLAB + s]
                    p_ref[d, s, idx, :] = p
                    new_p.append(p)
        return tuple(new_h), tuple(new_p)

    if chunked:
        h_init = tuple(jnp.zeros((N_SEG, LANES), F32) for _ in range(2 * N_SLAB))
        p_init = tuple(jnp.ones((N_SEG, LANES), F32) for _ in range(2 * N_SLAB))
    else:
        h0 = (h0f_ref[...], h0b_ref[...])
        h_init = tuple(h0[d][:, s * LANES:(s + 1) * LANES] for d in range(2) for s in range(N_SLAB))
        p_init = ()
    h_fin, _ = lax.fori_loop(0, SEG, step, (h_init, p_init), unroll=8)

    if chunked:
        b = pl.program_id(0)
        h0 = (h0f_ref[pl.ds(b, 1), :], h0b_ref[pl.ds(b, 1), :])
        for s in range(N_SLAB):
            lanes = slice(s * LANES, (s + 1) * LANES)
            carry = h0[0][:, lanes]
            for k in range(N_SEG):
                h = h_ref[0, s, _seg_rows(k), :] + p_ref[0, s, _seg_rows(k), :] * carry
                h_ref[0, s, _seg_rows(k), :] = h
                carry = h[SEG - 1:SEG, :]
            carry = h0[1][:, lanes]
            for k in reversed(range(N_SEG)):
                hb = h_ref[1, s, _seg_rows(k), :] + p_ref[1, s, _seg_rows(k), :] * carry
                carry = hb[0:1, :]
                rows = slice(k * SEG, (k + 1) * SEG)
                y = (h_ref[0, s, _seg_rows(k), :] + hb) * gg_ref[rows, lanes]
                y_ref[rows, lanes] = y.astype(BF16)
    else:
        for s in range(N_SLAB):
            lanes = slice(s * LANES, (s + 1) * LANES)
            sf_ref[:, lanes] = h_fin[s]
            sb_ref[:, lanes] = h_fin[N_SLAB + s]
            for k in range(N_SEG):
                rows = slice(k * SEG, (k + 1) * SEG)
                y = (h_ref[0, s, _seg_rows(k), :] + h_ref[1, s, _seg_rows(k), :]) * gg_ref[rows, lanes]
                y_ref[rows, lanes] = y.astype(BF16)


def _rg_scan(x, mod, norm_g, h0f, h0b, rg, *, layer, j, chunked, row, tile0, n_tiles):
    t = n_tiles * MIX_TT
    w_in, conv_w, conv_b, w_gate, b_a, b_x, lam = rg
    n_state = h0f.shape[0]
    state_block = (n_state, HP_W) if chunked else (N_SEG, HP_W)
    state_map = (lambda i, hp: (0, hp)) if chunked else (lambda i, hp: (i, hp))
    in_specs = [
        pl.BlockSpec((MIX_TT, D_MODEL), lambda i, hp: (tile0 + i, 0)),
        pl.BlockSpec((None, None, N_MOD, D_MODEL), lambda i, hp: (layer, row(tile0 + i), 0, 0)),
        pl.BlockSpec((None, 6, D_MODEL), lambda i, hp: (layer, 0, 0)),
        pl.BlockSpec(state_block, state_map),
        pl.BlockSpec(state_block, state_map),
        pl.BlockSpec((None, D_MODEL, HP_W), lambda i, hp: (j, 0, hp)),
        pl.BlockSpec((None, D_MODEL, HP_W), lambda i, hp: (j, 0, N_HP + hp)),
        pl.BlockSpec((None, CONV_W, HP_W), lambda i, hp: (j, 0, hp)),
        pl.BlockSpec((None, 1, HP_W), lambda i, hp: (j, 0, hp)),
        pl.BlockSpec((None, None, HP_W, 4 * HP_W), lambda i, hp: (j, hp, 0, 0)),
        pl.BlockSpec((None, 2, HP_W), lambda i, hp: (j, 0, hp)),
        pl.BlockSpec((None, 2, HP_W), lambda i, hp: (j, 0, hp)),
        pl.BlockSpec((None, 2, HP_W), lambda i, hp: (j, 0, hp)),
    ]
    y_spec = pl.BlockSpec((MIX_TT, HP_W), lambda i, hp: (i, hp))
    y_shape = jax.ShapeDtypeStruct((t, D_RNN), BF16)
    if chunked:
        out_specs, out_shape = y_spec, y_shape
    else:
        s_spec = pl.BlockSpec((N_SEG, HP_W), lambda i, hp: (i, hp))
        s_shape = jax.ShapeDtypeStruct((t // SEG, D_RNN), F32)
        out_specs, out_shape = (y_spec, s_spec, s_spec), (y_shape, s_shape, s_shape)
    scan = pltpu.VMEM((2, N_SLAB, SCAN_ROWS, LANES), F32)
    scratch = [
        pltpu.VMEM((MIX_TT, D_MODEL), BF16),
        pltpu.VMEM((MIX_TT, HP_W), F32),
        pltpu.VMEM((N_SLAB, SCAN_ROWS, LANES), F32),
        scan, scan, scan,
    ] + ([scan] if chunked else [])
    return pl.pallas_call(
        functools.partial(_rg_kernel, chunked=chunked),
        grid=(t // MIX_TT, N_HP),
        in_specs=in_specs,
        out_specs=out_specs,
        out_shape=out_shape,
        scratch_shapes=scratch,
        compiler_params=pltpu.CompilerParams(
            dimension_semantics=("parallel", "arbitrary"), vmem_limit_bytes=VMEM_LIMIT),
        name="rg_scan",
    )(x, mod, norm_g, h0f, h0b, w_in, w_in, conv_w, conv_b, w_gate, b_a, b_x, lam)


def _block_diag_pairs(w):
    n = w.shape[0]
    w = w.reshape(n, 2, N_HP, 2, RNN_HEAD_DIM, RNN_HEAD_DIM)
    z = jnp.zeros_like(w[:, :, :, 0])
    top = jnp.concatenate([w[:, :, :, 0], z], axis=-1)
    bot = jnp.concatenate([z, w[:, :, :, 1]], axis=-1)
    return jnp.concatenate([top, bot], axis=-2)


def _pool_layout(on_grid):
    run, n_run, pad_runs = (GRID_W, MIX_TT // GRID_W, POOL_HALO) if on_grid else (SEG, N_SEG, 0)
    pitch = run + POOL_HALO
    rows = 2 * POOL_EDGE + POOL_HALO + (n_run + 2 * pad_runs) * pitch
    return run, n_run, pad_runs, pitch, rows


def _window_count(half, pos, length):
    return (jnp.minimum(pos + half, length) - jnp.maximum(pos - half, 0)).astype(F32)


def _pool_kernel(x_ref, mod_ref, ng_ref, pw_ref, ps_ref, o_ref, inv_ref, d_ref, xb_ref, b0_ref, b1_ref,
                 *, n_ctx):
    i = pl.program_id(0)
    g = pl.program_id(1)

    @pl.when(g == 0)
    def _():
        x = x_ref[...]
        inv_ref[...] = lax.rsqrt(jnp.mean(x * x, axis=-1, keepdims=True) + EPS)

    row_id = lax.broadcasted_iota(jnp.int32, (MIX_TT, 1), 0)
    zero_edge = jnp.zeros((POOL_EDGE, LANES), F32)

    def pool_group(gi, w, on_grid):
        run, n_run, pad_runs, pitch, n_rows = _pool_layout(on_grid)
        base = lambda r: POOL_EDGE + POOL_HALO + (r + pad_runs) * pitch
        levels = w.bit_length() - 1
        half = w // 2

        def window_sums(s):
            src, bufs = xb_ref, [b0_ref, b1_ref]
            lo, hi = POOL_EDGE, n_rows - POOL_EDGE
            for lvl in range(levels):
                dst = bufs[lvl % 2]
                dn, up = (1, 0) if lvl == 0 else (2 ** (lvl - 1),) * 2
                dst[s, lo:hi, :] = src[s, lo - dn:hi - dn, :] + src[s, lo + up:hi + up, :]
                src = dst
            if on_grid:
                reach = 0
                for lvl in range(levels):
                    dst = bufs[(levels + lvl) % 2]
                    dn, up = (1, 0) if lvl == 0 else (2 ** (lvl - 1),) * 2
                    reach += dn
                    lo, hi = base(reach - pad_runs), base(n_run + pad_runs - reach)
                    dst[s, lo:hi, :] = (src[s, lo - dn * pitch:hi - dn * pitch, :]
                                        + src[s, lo + up * pitch:hi + up * pitch, :])
                    src = dst
            return src

        cols = slice(gi * POOL_GROUP_DIM, (gi + 1) * POOL_GROUP_DIM)
        scale = ng_ref[2:3, cols] * (1.0 + mod_ref[4:5, cols])
        xg = (x_ref[:, cols] * inv_ref[...]) * scale + mod_ref[3:4, cols]
        if on_grid:
            cnt = (_window_count(half, row_id % GRID_W, GRID_W)
                   * _window_count(half, row_id // GRID_W, n_run))
        else:
            cnt = _window_count(half, row_id % SEG, SEG)
        inv_cnt = 1.0 / cnt
        for s in range(POOL_GROUP_DIM // LANES):
            lanes = slice(s * LANES, (s + 1) * LANES)
            xb_ref[s, 0:n_rows, :] = jnp.zeros((n_rows, LANES), F32)
            for r in range(n_run):
                xb_ref[s, base(r):base(r) + run, :] = xg[r * run:(r + 1) * run, lanes]
            for buf in (b0_ref, b1_ref):
                buf[s, 0:POOL_EDGE, :] = zero_edge
                buf[s, n_rows - POOL_EDGE:n_rows, :] = zero_edge
            tot = window_sums(s)
            for r in range(n_run):
                rows = slice(r * run, (r + 1) * run)
                mean = tot[s, base(r):base(r) + run, :] * inv_cnt[rows, :]
                d_ref[rows, lanes] = (mean - xb_ref[s, base(r):base(r) + run, :]).astype(BF16)
        o_ref[:, cols] = _dot(d_ref[...], pw_ref[...].astype(BF16)) * ps_ref[...]

    for on_grid in (False, True):
        for gi, w in enumerate(POOL_WINDOWS):
            pl.when((g == gi) & ((i >= n_ctx) == on_grid))(functools.partial(pool_group, gi, w, on_grid))

    @pl.when(g == POOL_GROUPS - 1)
    def _():
        gain = mod_ref[5:6, :] * ng_ref[3:4, :]
        for r in range(MIX_TT // SEG):
            rows = slice(r * SEG, (r + 1) * SEG)
            y = o_ref[rows, :]
            inv = lax.rsqrt(jnp.mean(y * y, axis=-1, keepdims=True) + EPS)
            o_ref[rows, :] = x_ref[rows, :] + (y * inv) * gain


def _pool_mix(x, mod, norm_g, pool_w, pool_scale, *, layer, j, row, n_ctx):
    t = x.shape[0]
    n_rows = max(_pool_layout(False)[-1], _pool_layout(True)[-1])
    buf = pltpu.VMEM((POOL_GROUP_DIM // LANES, n_rows, LANES), F32)
    return pl.pallas_call(
        functools.partial(_pool_kernel, n_ctx=n_ctx),
        grid=(t // MIX_TT, POOL_GROUPS),
        in_specs=[
            pl.BlockSpec((MIX_TT, D_MODEL), lambda i, g: (i, 0)),
            pl.BlockSpec((None, None, N_MOD, D_MODEL), lambda i, g: (layer, row(i), 0, 0)),
            pl.BlockSpec((None, 6, D_MODEL), lambda i, g: (layer, 0, 0)),
            pl.BlockSpec((None, None, POOL_GROUP_DIM, POOL_GROUP_DIM), lambda i, g: (j, g, 0, 0)),
            pl.BlockSpec((None, 1, POOL_GROUP_DIM), lambda i, g: (j, 0, g)),
        ],
        out_specs=pl.BlockSpec((MIX_TT, D_MODEL), lambda i, g: (i, 0)),
        out_shape=jax.ShapeDtypeStruct((t, D_MODEL), F32),
        scratch_shapes=[
            pltpu.VMEM((MIX_TT, 1), F32),
            pltpu.VMEM((MIX_TT, POOL_GROUP_DIM), BF16),
            buf, buf, buf,
        ],
        compiler_params=pltpu.CompilerParams(
            dimension_semantics=("parallel", "arbitrary"), vmem_limit_bytes=VMEM_LIMIT),
        name="pool_mix",
    )(x, mod, norm_g, pool_w, pool_scale)


def kernel(x_prompt, x_sample, state_rglru, c, c_ctx, mod_w, mod_b, norm_g, ffn_w1, ffn_w3, ffn_w2,
           rg_w_in, rg_conv_w, rg_conv_b, rg_w_a, rg_b_a, rg_w_x, rg_b_x, rg_lam, rg_w_out,
           pool_w, pool_scale):
    batch, seq, _ = x_prompt.shape
    dec_batch, dec_seq, _ = x_sample.shape
    ctx_rows, lat_rows = batch * seq, dec_batch * dec_seq
    assert seq == SEG and dec_seq == MIX_TT and dec_seq % GRID_W == 0
    assert ctx_rows % MIX_TT == 0 and MIX_TT % FFN_TM == 0 and FFN_TM == RES_TM
    assert 1 + dec_batch <= MOD_ROWS
    n_rg = rg_w_in.shape[0]

    cond = jnp.zeros((MOD_ROWS, D_MODEL), F32).at[0].set(c_ctx).at[1:1 + dec_batch].set(c)
    mod = _modulation(cond, mod_w, mod_b).reshape(DEPTH, MOD_ROWS, N_MOD, D_MODEL)

    wa, wx = _block_diag_pairs(rg_w_a), _block_diag_pairs(rg_w_x)
    w_gate = (0.5 * jnp.concatenate([wa[:, 0], wx[:, 0], wa[:, 1], wx[:, 1]], axis=-1)).astype(BF16)
    rg = (rg_w_in, rg_conv_w, rg_conv_b.reshape(n_rg, 1, D_RNN), w_gate, rg_b_a, rg_b_x, rg_lam)
    pool_scale3 = pool_scale.reshape(pool_scale.shape[0], 1, D_MODEL)
    zeros = jnp.zeros((batch, D_RNN), F32)
    ffn_w1, ffn_w3, ffn_w2 = (w.astype(BF16) for w in (ffn_w1, ffn_w3, ffn_w2))

    x = jnp.concatenate([x_prompt.reshape(ctx_rows, D_MODEL), x_sample.reshape(lat_rows, D_MODEL)], axis=0)
    ffn_row = _mod_row_map(FFN_TM, ctx_rows, dec_seq)
    mix_row = _mod_row_map(MIX_TT, ctx_rows, dec_seq)
    n_ffn_ctx, n_ffn_lat = ctx_rows // FFN_TM, lat_rows // FFN_TM
    n_mix_ctx, n_mix_lat = ctx_rows // MIX_TT, lat_rows // MIX_TT
    ffn = functools.partial(_ffn, mod=mod, norm_g=norm_g, w1=ffn_w1, w3=ffn_w3, w2=ffn_w2, row=ffn_row)

    ctx_states = []
    for l in range(DEPTH):
        j = l // 2
        x = ffn(x, layer=l, stage=0, tile0=0, n_tiles=n_ffn_ctx + n_ffn_lat)
        if l % 2 == 0:
            scan = functools.partial(_rg_scan, x, mod, norm_g, rg=rg, layer=l, j=j, row=mix_row)
            y_ctx, sf, sb = scan(zeros, zeros, chunked=False, tile0=0, n_tiles=n_mix_ctx)
            y_lat = scan(state_rglru[:, j, 0], state_rglru[:, j, 1], chunked=True,
                         tile0=n_mix_ctx, n_tiles=n_mix_lat)
            ctx_states.append(jnp.stack([sf, sb], axis=1))
            x = _out_proj(x, y_ctx, y_lat, mod, norm_g, rg_w_out, layer=l, j=j, row=ffn_row)
        else:
            x = _pool_mix(x, mod, norm_g, pool_w, pool_scale3, layer=l, j=j, row=mix_row, n_ctx=n_mix_ctx)
        if l + 1 < DEPTH:
            x = ffn(x, layer=l, stage=2, tile0=0, n_tiles=n_ffn_ctx + n_ffn_lat)
    y_prompt = ffn(x, layer=DEPTH - 1, stage=2, tile0=0, n_tiles=n_ffn_ctx)
    y_sample = ffn(x, layer=DEPTH - 1, stage=2, tile0=n_ffn_ctx, n_tiles=n_ffn_lat)
    new_state = jnp.stack(ctx_states, axis=1)
    return (y_prompt.reshape(batch, seq, D_MODEL), y_sample.reshape(dec_batch, dec_seq, D_MODEL), new_state)
```

```python
import functools

import jax
import jax.numpy as jnp
from jax import lax
from jax.experimental import pallas as pl
from jax.experimental.pallas import tpu as pltpu

D_MODEL = 1024
DEPTH = 4
D_RNN = 1024
RNN_HEADS = 8
RNN_HEAD_DIM = D_RNN // RNN_HEADS
CONV_W = 4
RG_C = 8.0
GRID_W = 64
POOL_WINDOWS = (2, 4, 8, 16)
POOL_GROUPS = 4
POOL_GROUP_DIM = D_MODEL // POOL_GROUPS
D_FF = 2816
N_MOD = 9
EPS = 1e-6

F32 = jnp.float32
BF16 = jnp.bfloat16

MOD_ROWS = 8
MOD_TN = 2304
FFN_TM = 1024
FFN_TF = 256
FFN_TN = 256
FFN_RB = 256
FFN_NA = D_FF // FFN_TF
FFN_NB = D_MODEL // FFN_TN
RES_TM = 1024
SEG = 256
N_SEG = 8
MIX_TT = SEG * N_SEG
SEG_GAP = 4
SEG_PITCH = SEG + SEG_GAP
SCAN_ROWS = SEG_GAP + N_SEG * SEG_PITCH + 4
RG_CHUNK = 2 * SEG
HP_W = 2 * RNN_HEAD_DIM
N_HP = D_RNN // HP_W
LANES = 128
N_SLAB = HP_W // LANES
POOL_HALO = max(POOL_WINDOWS) // 2
POOL_EDGE = 8
VMEM_LIMIT = 60 * 1024 * 1024


def _dot(a, b):
    return jnp.dot(a, b, preferred_element_type=F32)


def _rms(x, g):
    ms = jnp.mean(x * x, axis=-1, keepdims=True)
    return x * lax.rsqrt(ms + EPS) * g


def _sigmoid(x):
    return 0.5 * jnp.tanh(0.5 * x) + 0.5


def _norm_modulate(x, ng_ref, mod_ref, stage):
    g = ng_ref[2 * stage:2 * stage + 1, :]
    shift = mod_ref[3 * stage:3 * stage + 1, :]
    scale = mod_ref[3 * stage + 1:3 * stage + 2, :]
    inv = lax.rsqrt(jnp.mean(x * x, axis=-1, keepdims=True) + EPS)
    return (x * inv) * (g * (1.0 + scale)) + shift


def _mod_row_map(tile_rows, ctx_rows, lat_rows):
    def row(tile):
        start = tile * tile_rows
        return jnp.where(start < ctx_rows, 0, 1 + (start - ctx_rows) // lat_rows)
    return row


def _mod_kernel(c_ref, w_ref, b_ref, o_ref):
    c = c_ref[...]
    s = c * _sigmoid(c)
    o_ref[...] = _dot(s.astype(BF16), w_ref[...].astype(BF16)) + b_ref[...]


def _modulation(cond, mod_w, mod_b):
    tn = MOD_TN
    n_col = (N_MOD * D_MODEL) // tn
    return pl.pallas_call(
        _mod_kernel,
        grid=(DEPTH, n_col),
        in_specs=[
            pl.BlockSpec((MOD_ROWS, D_MODEL), lambda l, n: (0, 0)),
            pl.BlockSpec((None, D_MODEL, tn), lambda l, n: (l, 0, n)),
            pl.BlockSpec((None, 1, tn), lambda l, n: (l, 0, n)),
        ],
        out_specs=pl.BlockSpec((None, MOD_ROWS, tn), lambda l, n: (l, 0, n)),
        out_shape=jax.ShapeDtypeStruct((DEPTH, MOD_ROWS, N_MOD * D_MODEL), F32),
        compiler_params=pltpu.CompilerParams(
            dimension_semantics=("arbitrary", "arbitrary"), vmem_limit_bytes=VMEM_LIMIT),
        name="modulation",
    )(cond, mod_w, mod_b.reshape(DEPTH, 1, N_MOD * D_MODEL))


def _ffn_kernel(x_ref, xn_ref, mod_ref, modn_ref, ng_ref, w1_ref, w3_ref, w2_ref, o_ref, hn_ref, p_ref,
                *, stage):
    @pl.when(pl.program_id(0) == 0)
    def _():
        hn_ref[...] = _norm_modulate(x_ref[...], ng_ref, mod_ref, stage).astype(BF16)

    gain = ng_ref[2 * stage + 1:2 * stage + 2, :] * (0.5 * mod_ref[3 * stage + 2:3 * stage + 3, :])
    n_blocks = FFN_TM // FFN_RB
    block_rows = [slice(r * FFN_RB, (r + 1) * FFN_RB) for r in range(n_blocks)]

    def up_project(rows):
        for k in range(FFN_NA):
            cols = slice(k * FFN_TF, (k + 1) * FFN_TF)
            a = _dot(hn_ref[rows, :], w1_ref[:, cols])
            b = _dot(hn_ref[rows, :], w3_ref[:, cols])
            p_ref[k, rows, :] = (a * _sigmoid(a) * b).astype(BF16)

    def down_project(r):
        rows = block_rows[r]
        piece = FFN_RB // FFN_NB
        accs = []
        for n in range(FFN_NB):
            nrows = slice(r * FFN_RB + n * piece, r * FFN_RB + (n + 1) * piece)
            hn_ref[nrows, :] = _norm_modulate(xn_ref[nrows, :], ng_ref, modn_ref, stage).astype(BF16)
            cols = slice(n * FFN_TN, (n + 1) * FFN_TN)
            acc = _dot(p_ref[0, rows, :], w2_ref[0:FFN_TF, cols])
            for k in range(1, FFN_NA):
                acc += _dot(p_ref[k, rows, :], w2_ref[k * FFN_TF:(k + 1) * FFN_TF, cols])
            accs.append(acc)
        ss = sum(jnp.sum(acc * acc, axis=-1, keepdims=True) for acc in accs)
        inv = lax.rsqrt(ss * (1.0 / D_MODEL) + EPS)
        for n, acc in enumerate(accs):
            cols = slice(n * FFN_TN, (n + 1) * FFN_TN)
            o_ref[rows, cols] = x_ref[rows, cols] + (acc * inv) * gain[:, cols]

    up_project(block_rows[0])
    for r in range(n_blocks):
        if r + 1 < n_blocks:
            up_project(block_rows[r + 1])
        down_project(r)


def _ffn(x, mod, norm_g, w1, w3, w2, *, layer, stage, row, tile0, n_tiles):
    sub = 0 if stage == 0 else 1
    cur = lambda i: tile0 + i
    nxt = lambda i: tile0 + jnp.minimum(i + 1, n_tiles - 1)
    resident = dict(pipeline_mode=pl.Buffered(1))
    return pl.pallas_call(
        functools.partial(_ffn_kernel, stage=stage),
        grid=(n_tiles,),
        in_specs=[
            pl.BlockSpec((FFN_TM, D_MODEL), lambda i: (cur(i), 0)),
            pl.BlockSpec((FFN_TM, D_MODEL), lambda i: (nxt(i), 0)),
            pl.BlockSpec((None, None, N_MOD, D_MODEL), lambda i: (layer, row(cur(i)), 0, 0)),
            pl.BlockSpec((None, None, N_MOD, D_MODEL), lambda i: (layer, row(nxt(i)), 0, 0)),
            pl.BlockSpec((None, 6, D_MODEL), lambda i: (layer, 0, 0)),
            pl.BlockSpec((None, None, D_MODEL, D_FF), lambda i: (layer, sub, 0, 0), **resident),
            pl.BlockSpec((None, None, D_MODEL, D_FF), lambda i: (layer, sub, 0, 0), **resident),
            pl.BlockSpec((None, None, D_FF, D_MODEL), lambda i: (layer, sub, 0, 0), **resident),
        ],
        out_specs=pl.BlockSpec((FFN_TM, D_MODEL), lambda i: (i, 0)),
        out_shape=jax.ShapeDtypeStruct((n_tiles * FFN_TM, D_MODEL), F32),
        scratch_shapes=[
            pltpu.VMEM((FFN_TM, D_MODEL), BF16),
            pltpu.VMEM((FFN_NA, FFN_TM, FFN_TF), BF16),
        ],
        compiler_params=pltpu.CompilerParams(
            dimension_semantics=("arbitrary",), vmem_limit_bytes=VMEM_LIMIT),
        name="ffn",
    )(x, x, mod, mod, norm_g, w1, w3, w2)


def _out_proj_kernel(x_ref, yc_ref, yl_ref, mod_ref, ng_ref, w_ref, o_ref, *, n_ctx):
    y_in = jnp.where(pl.program_id(0) < n_ctx, yc_ref[...], yl_ref[...])
    y = _dot(y_in, w_ref[...])
    o_ref[...] = x_ref[...] + mod_ref[5:6, :] * _rms(y, ng_ref[3:4, :])


def _out_proj(x, y_ctx, y_lat, mod, norm_g, w_out, *, layer, j, row):
    n_ctx = y_ctx.shape[0] // RES_TM
    n_lat = y_lat.shape[0] // RES_TM
    return pl.pallas_call(
        functools.partial(_out_proj_kernel, n_ctx=n_ctx),
        grid=(n_ctx + n_lat,),
        in_specs=[
            pl.BlockSpec((RES_TM, D_MODEL), lambda i: (i, 0)),
            pl.BlockSpec((RES_TM, D_RNN), lambda i: (jnp.minimum(i, n_ctx - 1), 0)),
            pl.BlockSpec((RES_TM, D_RNN), lambda i: (jnp.maximum(i - n_ctx, 0), 0)),
            pl.BlockSpec((None, None, N_MOD, D_MODEL), lambda i: (layer, row(i), 0, 0)),
            pl.BlockSpec((None, 6, D_MODEL), lambda i: (layer, 0, 0)),
            pl.BlockSpec((None, D_RNN, D_MODEL), lambda i: (j, 0, 0)),
        ],
        out_specs=pl.BlockSpec((RES_TM, D_MODEL), lambda i: (i, 0)),
        out_shape=jax.ShapeDtypeStruct(x.shape, F32),
        compiler_params=pltpu.CompilerParams(
            dimension_semantics=("parallel",), vmem_limit_bytes=VMEM_LIMIT),
        name="out_proj",
    )(x, y_ctx, y_lat, mod, norm_g, w_out)


def _seg_base(k):
    return SEG_GAP + k * SEG_PITCH


def _seg_rows(k, offset=0):
    return pl.ds(_seg_base(k) + offset, SEG)


def _rg_kernel(x_ref, mod_ref, ng_ref, h0f_ref, h0b_ref, wig_ref, wix_ref, cw_ref, cb_ref, wg_ref,
               ba_ref, bx_ref, lam_ref, *rest, chunked):
    if chunked:
        y_ref, hn_ref, gg_ref, xr_ref, a_ref, u_ref, h_ref, p_ref = rest
    else:
        y_ref, sf_ref, sb_ref, hn_ref, gg_ref, xr_ref, a_ref, u_ref, h_ref = rest
    hp = pl.program_id(1)

    @pl.when(hp == 0)
    def _():
        hn_ref[...] = _norm_modulate(x_ref[...], ng_ref, mod_ref, 1).astype(BF16)

    wig = wig_ref[...]
    wix = wix_ref[...]
    seg_per_chunk = RG_CHUNK // SEG
    zero2 = jnp.zeros((2, LANES), F32)

    def project(c):
        rows = slice(c * RG_CHUNK, (c + 1) * RG_CHUNK)
        hn = hn_ref[rows, :]
        gg_ref[rows, :] = jax.nn.gelu(_dot(hn, wig))
        xr = _dot(hn, wix)
        for s in range(N_SLAB):
            lanes = slice(s * LANES, (s + 1) * LANES)
            for kk in range(seg_per_chunk):
                k = c * seg_per_chunk + kk
                base = _seg_base(k)
                seg = xr[kk * SEG:(kk + 1) * SEG, lanes]
                xr_ref[s, _seg_rows(k), :] = seg
                inside = chunked and k > 0
                xr_ref[s, base - SEG_GAP:base - SEG_GAP + 2, :] = seg[0:2, :] if inside else zero2
                inside = chunked and k < N_SEG - 1
                xr_ref[s, base + SEG_PITCH - 2:base + SEG_PITCH, :] = seg[SEG - 2:SEG, :] if inside else zero2
                if k == 0:
                    xr_ref[s, base - 2:base, :] = zero2
                if k == N_SEG - 1:
                    xr_ref[s, base + SEG:base + SEG + 2, :] = zero2

    z = -lam_ref[...]
    softplus = jnp.maximum(z, 0.0) + jnp.log1p(jnp.exp(-jnp.abs(z)))
    half_rate = (0.5 * (-RG_C)) * softplus
    half_ba = 0.5 * ba_ref[...]
    half_bx = 0.5 * bx_ref[...]
    cw = cw_ref[...]
    cb = cb_ref[...]

    def gates(k):
        taps = []
        for s in range(N_SLAB):
            lanes = slice(s * LANES, (s + 1) * LANES)
            acc = cb[:, lanes] + cw[0:1, lanes] * xr_ref[s, _seg_rows(k, -1), :]
            for j in range(1, CONV_W):
                acc += cw[j:j + 1, lanes] * xr_ref[s, _seg_rows(k, j - 1), :]
            taps.append(acc)
        xk = jnp.concatenate(taps, axis=1)
        pre = _dot(xk.astype(BF16), wg_ref[...])
        xh = 0.5 * xk
        for d in range(2):
            tr = jnp.tanh(pre[:, (2 * d) * HP_W:(2 * d + 1) * HP_W] + half_ba[d:d + 1, :])
            ti = jnp.tanh(pre[:, (2 * d + 1) * HP_W:(2 * d + 2) * HP_W] + half_bx[d:d + 1, :])
            a = jnp.exp(tr * half_rate[d:d + 1, :] + half_rate[d:d + 1, :])
            v = 1.0 - a * a
            beta = jnp.where(v > 0.0, v * lax.rsqrt(v), 0.0)
            u = beta * (xh * ti + xh)
            for s in range(N_SLAB):
                lanes = slice(s * LANES, (s + 1) * LANES)
                a_ref[d, s, _seg_rows(k), :] = a[:, lanes]
                u_ref[d, s, _seg_rows(k), :] = u[:, lanes]

    n_chunk = MIX_TT // RG_CHUNK
    project(0)
    for c in range(n_chunk):
        if c + 1 < n_chunk:
            project(c + 1)
        first = c * seg_per_chunk - 1
        for k in range(max(first, 0), first + seg_per_chunk):
            gates(k)
    gates(N_SEG - 1)

    def step(t, carry):
        hs, ps = carry
        new_h, new_p = [], []
        for d in range(2):
            row = SEG_GAP + (t if d == 0 else SEG - 1 - t)
            idx = pl.ds(row, N_SEG, stride=SEG_PITCH)
            for s in range(N_SLAB):
                a = a_ref[d, s, idx, :]
                h = a * hs[d * N_SLAB + s] + u_ref[d, s, idx, :]
                h_ref[d, s, idx, :] = h
                new_h.append(h)
                if chunked:
                    p = a * ps[d * N_SLAB + s]
                    p_ref[d, s, idx, :] = p
                    new_p.append(p)
        return tuple(new_h), tuple(new_p)

    if chunked:
        h_init = tuple(jnp.zeros((N_SEG, LANES), F32) for _ in range(2 * N_SLAB))
        p_init = tuple(jnp.ones((N_SEG, LANES), F32) for _ in range(2 * N_SLAB))
    else:
        h0 = (h0f_ref[...], h0b_ref[...])
        h_init = tuple(h0[d][:, s * LANES:(s + 1) * LANES] for d in range(2) for s in range(N_SLAB))
        p_init = ()
    h_fin, _ = lax.fori_loop(0, SEG, step, (h_init, p_init), unroll=8)

    if chunked:
        b = pl.program_id(0)
        h0 = (h0f_ref[pl.ds(b, 1), :], h0b_ref[pl.ds(b, 1), :])
        for s in range(N_SLAB):
            lanes = slice(s * LANES, (s + 1) * LANES)
            carry = h0[0][:, lanes]
            for k in range(N_SEG):
                h = h_ref[0, s, _seg_rows(k), :] + p_ref[0, s, _seg_rows(k), :] * carry
                h_ref[0, s, _seg_rows(k), :] = h
                carry = h[SEG - 1:SEG, :]
            carry = h0[1][:, lanes]
            for k in reversed(range(N_SEG)):
                hb = h_ref[1, s, _seg_rows(k), :] + p_ref[1, s, _seg_rows(k), :] * carry
                carry = hb[0:1, :]
                rows = slice(k * SEG, (k + 1) * SEG)
                y = (h_ref[0, s, _seg_rows(k), :] + hb) * gg_ref[rows, lanes]
                y_ref[rows, lanes] = y.astype(BF16)
    else:
        for s in range(N_SLAB):
            lanes = slice(s * LANES, (s + 1) * LANES)
            sf_ref[:, lanes] = h_fin[s]
            sb_ref[:, lanes] = h_fin[N_SLAB + s]
            for k in range(N_SEG):
                rows = slice(k * SEG, (k + 1) * SEG)
                y = (h_ref[0, s, _seg_rows(k), :] + h_ref[1, s, _seg_rows(k), :]) * gg_ref[rows, lanes]
                y_ref[rows, lanes] = y.astype(BF16)


def _rg_scan(x, mod, norm_g, h0f, h0b, rg, *, layer, j, chunked, row, tile0, n_tiles):
    t = n_tiles * MIX_TT
    w_in, conv_w, conv_b, w_gate, b_a, b_x, lam = rg
    n_state = h0f.shape[0]
    state_block = (n_state, HP_W) if chunked else (N_SEG, HP_W)
    state_map = (lambda i, hp: (0, hp)) if chunked else (lambda i, hp: (i, hp))
    in_specs = [
        pl.BlockSpec((MIX_TT, D_MODEL), lambda i, hp: (tile0 + i, 0)),
        pl.BlockSpec((None, None, N_MOD, D_MODEL), lambda i, hp: (layer, row(tile0 + i), 0, 0)),
        pl.BlockSpec((None, 6, D_MODEL), lambda i, hp: (layer, 0, 0)),
        pl.BlockSpec(state_block, state_map),
        pl.BlockSpec(state_block, state_map),
        pl.BlockSpec((None, D_MODEL, HP_W), lambda i, hp: (j, 0, hp)),
        pl.BlockSpec((None, D_MODEL, HP_W), lambda i, hp: (j, 0, N_HP + hp)),
        pl.BlockSpec((None, CONV_W, HP_W), lambda i, hp: (j, 0, hp)),
        pl.BlockSpec((None, 1, HP_W), lambda i, hp: (j, 0, hp)),
        pl.BlockSpec((None, None, HP_W, 4 * HP_W), lambda i, hp: (j, hp, 0, 0)),
        pl.BlockSpec((None, 2, HP_W), lambda i, hp: (j, 0, hp)),
        pl.BlockSpec((None, 2, HP_W), lambda i, hp: (j, 0, hp)),
        pl.BlockSpec((None, 2, HP_W), lambda i, hp: (j, 0, hp)),
    ]
    y_spec = pl.BlockSpec((MIX_TT, HP_W), lambda i, hp: (i, hp))
    y_shape = jax.ShapeDtypeStruct((t, D_RNN), BF16)
    if chunked:
        out_specs, out_shape = y_spec, y_shape
    else:
        s_spec = pl.BlockSpec((N_SEG, HP_W), lambda i, hp: (i, hp))
        s_shape = jax.ShapeDtypeStruct((t // SEG, D_RNN), F32)
        out_specs, out_shape = (y_spec, s_spec, s_spec), (y_shape, s_shape, s_shape)
    scan = pltpu.VMEM((2, N_SLAB, SCAN_ROWS, LANES), F32)
    scratch = [
        pltpu.VMEM((MIX_TT, D_MODEL), BF16),
        pltpu.VMEM((MIX_TT, HP_W), F32),
        pltpu.VMEM((N_SLAB, SCAN_ROWS, LANES), F32),
        scan, scan, scan,
    ] + ([scan] if chunked else [])
    return pl.pallas_call(
        functools.partial(_rg_kernel, chunked=chunked),
        grid=(t // MIX_TT, N_HP),
        in_specs=in_specs,
        out_specs=out_specs,
        out_shape=out_shape,
        scratch_shapes=scratch,
        compiler_params=pltpu.CompilerParams(
            dimension_semantics=("parallel", "arbitrary"), vmem_limit_bytes=VMEM_LIMIT),
        name="rg_scan",
    )(x, mod, norm_g, h0f, h0b, w_in, w_in, conv_w, conv_b, w_gate, b_a, b_x, lam)


def _block_diag_pairs(w):
    n = w.shape[0]
    w = w.reshape(n, 2, N_HP, 2, RNN_HEAD_DIM, RNN_HEAD_DIM)
    z = jnp.zeros_like(w[:, :, :, 0])
    top = jnp.concatenate([w[:, :, :, 0], z], axis=-1)
    bot = jnp.concatenate([z, w[:, :, :, 1]], axis=-1)
    return jnp.concatenate([top, bot], axis=-2)


def _pool_layout(on_grid):
    run, n_run, pad_runs = (GRID_W, MIX_TT // GRID_W, POOL_HALO) if on_grid else (SEG, N_SEG, 0)
    pitch = run + POOL_HALO
    rows = 2 * POOL_EDGE + POOL_HALO + (n_run + 2 * pad_runs) * pitch
    return run, n_run, pad_runs, pitch, rows


def _window_count(half, pos, length):
    return (jnp.minimum(pos + half, length) - jnp.maximum(pos - half, 0)).astype(F32)


def _pool_kernel(x_ref, mod_ref, ng_ref, pw_ref, ps_ref, o_ref, inv_ref, d_ref, xb_ref, b0_ref, b1_ref,
                 *, n_ctx):
    i = pl.program_id(0)
    g = pl.program_id(1)

    @pl.when(g == 0)
    def _():
        x = x_ref[...]
        inv_ref[...] = lax.rsqrt(jnp.mean(x * x, axis=-1, keepdims=True) + EPS)

    row_id = lax.broadcasted_iota(jnp.int32, (MIX_TT, 1), 0)
    zero_edge = jnp.zeros((POOL_EDGE, LANES), F32)

    def pool_group(gi, w, on_grid):
        run, n_run, pad_runs, pitch, n_rows = _pool_layout(on_grid)
        base = lambda r: POOL_EDGE + POOL_HALO + (r + pad_runs) * pitch
        levels = w.bit_length() - 1
        half = w // 2

        def window_sums(s):
            src, bufs = xb_ref, [b0_ref, b1_ref]
            lo, hi = POOL_EDGE, n_rows - POOL_EDGE
            for lvl in range(levels):
                dst = bufs[lvl % 2]
                dn, up = (1, 0) if lvl == 0 else (2 ** (lvl - 1),) * 2
                dst[s, lo:hi, :] = src[s, lo - dn:hi - dn, :] + src[s, lo + up:hi + up, :]
                src = dst
            if on_grid:
                reach = 0
                for lvl in range(levels):
                    dst = bufs[(levels + lvl) % 2]
                    dn, up = (1, 0) if lvl == 0 else (2 ** (lvl - 1),) * 2
                    reach += dn
                    lo, hi = base(reach - pad_runs), base(n_run + pad_runs - reach)
                    dst[s, lo:hi, :] = (src[s, lo - dn * pitch:hi - dn * pitch, :]
                                        + src[s, lo + up * pitch:hi + up * pitch, :])
                    src = dst
            return src

        cols = slice(gi * POOL_GROUP_DIM, (gi + 1) * POOL_GROUP_DIM)
        scale = ng_ref[2:3, cols] * (1.0 + mod_ref[4:5, cols])
        xg = (x_ref[:, cols] * inv_ref[...]) * scale + mod_ref[3:4, cols]
        if on_grid:
            cnt = (_window_count(half, row_id % GRID_W, GRID_W)
                   * _window_count(half, row_id // GRID_W, n_run))
        else:
            cnt = _window_count(half, row_id % SEG, SEG)
        inv_cnt = 1.0 / cnt
        for s in range(POOL_GROUP_DIM // LANES):
            lanes = slice(s * LANES, (s + 1) * LANES)
            xb_ref[s, 0:n_rows, :] = jnp.zeros((n_rows, LANES), F32)
            for r in range(n_run):
                xb_ref[s, base(r):base(r) + run, :] = xg[r * run:(r + 1) * run, lanes]
            for buf in (b0_ref, b1_ref):
                buf[s, 0:POOL_EDGE, :] = zero_edge
                buf[s, n_rows - POOL_EDGE:n_rows, :] = zero_edge
            tot = window_sums(s)
            for r in range(n_run):
                rows = slice(r * run, (r + 1) * run)
                mean = tot[s, base(r):base(r) + run, :] * inv_cnt[rows, :]
                d_ref[rows, lanes] = (mean - xb_ref[s, base(r):base(r) + run, :]).astype(BF16)
        o_ref[:, cols] = _dot(d_ref[...], pw_ref[...]) * ps_ref[...]

    for on_grid in (False, True):
        for gi, w in enumerate(POOL_WINDOWS):
            pl.when((g == gi) & ((i >= n_ctx) == on_grid))(functools.partial(pool_group, gi, w, on_grid))

    @pl.when(g == POOL_GROUPS - 1)
    def _():
        gain = mod_ref[5:6, :] * ng_ref[3:4, :]
        for r in range(MIX_TT // SEG):
            rows = slice(r * SEG, (r + 1) * SEG)
            y = o_ref[rows, :]
            inv = lax.rsqrt(jnp.mean(y * y, axis=-1, keepdims=True) + EPS)
            o_ref[rows, :] = x_ref[rows, :] + (y * inv) * gain


def _pool_mix(x, mod, norm_g, pool_w, pool_scale, *, layer, j, row, n_ctx):
    t = x.shape[0]
    n_rows = max(_pool_layout(False)[-1], _pool_layout(True)[-1])
    buf = pltpu.VMEM((POOL_GROUP_DIM // LANES, n_rows, LANES), F32)
    return pl.pallas_call(
        functools.partial(_pool_kernel, n_ctx=n_ctx),
        grid=(t // MIX_TT, POOL_GROUPS),
        in_specs=[
            pl.BlockSpec((MIX_TT, D_MODEL), lambda i, g: (i, 0)),
            pl.BlockSpec((None, None, N_MOD, D_MODEL), lambda i, g: (layer, row(i), 0, 0)),
            pl.BlockSpec((None, 6, D_MODEL), lambda i, g: (layer, 0, 0)),
            pl.BlockSpec((None, None, POOL_GROUP_DIM, POOL_GROUP_DIM), lambda i, g: (j, g, 0, 0)),
            pl.BlockSpec((None, 1, POOL_GROUP_DIM), lambda i, g: (j, 0, g)),
        ],
        out_specs=pl.BlockSpec((MIX_TT, D_MODEL), lambda i, g: (i, 0)),
        out_shape=jax.ShapeDtypeStruct((t, D_MODEL), F32),
        scratch_shapes=[
            pltpu.VMEM((MIX_TT, 1), F32),
            pltpu.VMEM((MIX_TT, POOL_GROUP_DIM), BF16),
            buf, buf, buf,
        ],
        compiler_params=pltpu.CompilerParams(
            dimension_semantics=("parallel", "arbitrary"), vmem_limit_bytes=VMEM_LIMIT),
        name="pool_mix",
    )(x, mod, norm_g, pool_w, pool_scale)


def kernel(x_prompt, x_sample, state_rglru, c, c_ctx, mod_w, mod_b, norm_g, ffn_w1, ffn_w3, ffn_w2,
           rg_w_in, rg_conv_w, rg_conv_b, rg_w_a, rg_b_a, rg_w_x, rg_b_x, rg_lam, rg_w_out,
           pool_w, pool_scale):
    batch, seq, _ = x_prompt.shape
    dec_batch, dec_seq, _ = x_sample.shape
    ctx_rows, lat_rows = batch * seq, dec_batch * dec_seq
    assert seq == SEG and dec_seq == MIX_TT and dec_seq % GRID_W == 0
    assert ctx_rows % MIX_TT == 0 and MIX_TT % FFN_TM == 0 and FFN_TM == RES_TM
    assert 1 + dec_batch <= MOD_ROWS
    n_rg = rg_w_in.shape[0]

    cond = jnp.zeros((MOD_ROWS, D_MODEL), F32).at[0].set(c_ctx).at[1:1 + dec_batch].set(c)
    mod = _modulation(cond, mod_w, mod_b).reshape(DEPTH, MOD_ROWS, N_MOD, D_MODEL)

    wa, wx = _block_diag_pairs(rg_w_a), _block_diag_pairs(rg_w_x)
    w_gate = (0.5 * jnp.concatenate([wa[:, 0], wx[:, 0], wa[:, 1], wx[:, 1]], axis=-1)).astype(BF16)
    rg = (rg_w_in.astype(BF16), rg_conv_w, rg_conv_b.reshape(n_rg, 1, D_RNN), w_gate, rg_b_a, rg_b_x, rg_lam)
    pool_scale3 = pool_scale.reshape(pool_scale.shape[0], 1, D_MODEL)
    zeros = jnp.zeros((batch, D_RNN), F32)
    ffn_w1, ffn_w3, ffn_w2, rg_w_out, pool_w = (
        w.astype(BF16) for w in (ffn_w1, ffn_w3, ffn_w2, rg_w_out, pool_w))

    x = jnp.concatenate([x_prompt.reshape(ctx_rows, D_MODEL), x_sample.reshape(lat_rows, D_MODEL)], axis=0)
    ffn_row = _mod_row_map(FFN_TM, ctx_rows, dec_seq)
    mix_row = _mod_row_map(MIX_TT, ctx_rows, dec_seq)
    n_ffn_ctx, n_ffn_lat = ctx_rows // FFN_TM, lat_rows // FFN_TM
    n_mix_ctx, n_mix_lat = ctx_rows // MIX_TT, lat_rows // MIX_TT
    ffn = functools.partial(_ffn, mod=mod, norm_g=norm_g, w1=ffn_w1, w3=ffn_w3, w2=ffn_w2, row=ffn_row)

    ctx_states = []
    for l in range(DEPTH):
        j = l // 2
        x = ffn(x, layer=l, stage=0, tile0=0, n_tiles=n_ffn_ctx + n_ffn_lat)
        if l % 2 == 0:
            scan = functools.partial(_rg_scan, x, mod, norm_g, rg=rg, layer=l, j=j, row=mix_row)
            y_ctx, sf, sb = scan(zeros, zeros, chunked=False, tile0=0, n_tiles=n_mix_ctx)
            y_lat = scan(state_rglru[:, j, 0], state_rglru[:, j, 1], chunked=True,
                         tile0=n_mix_ctx, n_tiles=n_mix_lat)
            ctx_states.append(jnp.stack([sf, sb], axis=1))
            x = _out_proj(x, y_ctx, y_lat, mod, norm_g, rg_w_out, layer=l, j=j, row=ffn_row)
        else:
            x = _pool_mix(x, mod, norm_g, pool_w, pool_scale3, layer=l, j=j, row=mix_row, n_ctx=n_mix_ctx)
        if l + 1 < DEPTH:
            x = ffn(x, layer=l, stage=2, tile0=0, n_tiles=n_ffn_ctx + n_ffn_lat)
    y_prompt = ffn(x, layer=DEPTH - 1, stage=2, tile0=0, n_tiles=n_ffn_ctx)
    y_sample = ffn(x, layer=DEPTH - 1, stage=2, tile0=n_ffn_ctx, n_tiles=n_ffn_lat)
    new_state = jnp.stack(ctx_states, axis=1)
    return (y_prompt.reshape(batch, seq, D_MODEL), y_sample.reshape(dec_batch, dec_seq, D_MODEL), new_state)
```

```python
import functools

import jax
import jax.numpy as jnp
from jax import lax
from jax.experimental import pallas as pl
from jax.experimental.pallas import tpu as pltpu

D_MODEL = 1024
DEPTH = 4
D_RNN = 1024
RNN_HEADS = 8
RNN_HEAD_DIM = D_RNN // RNN_HEADS
CONV_W = 4
RG_C = 8.0
GRID_W = 64
POOL_WINDOWS = (2, 4, 8, 16)
POOL_GROUPS = 4
POOL_GROUP_DIM = D_MODEL // POOL_GROUPS
D_FF = 2816
N_MOD = 9
EPS = 1e-6

F32 = jnp.float32
BF16 = jnp.bfloat16

MOD_ROWS = 8
MOD_TN = 2304
FFN_TM = 1024
FFN_TF = 256
FFN_TN = 256
FFN_RB = 256
FFN_NA = D_FF // FFN_TF
FFN_NB = D_MODEL // FFN_TN
RES_TM = 1024
SEG = 256
N_SEG = 8
MIX_TT = SEG * N_SEG
SEG_GAP = 4
SEG_PITCH = SEG + SEG_GAP
SCAN_ROWS = SEG_GAP + N_SEG * SEG_PITCH + 4
RG_CHUNK = 2 * SEG
HP_W = 2 * RNN_HEAD_DIM
N_HP = D_RNN // HP_W
LANES = 128
N_SLAB = HP_W // LANES
POOL_HALO = max(POOL_WINDOWS) // 2
POOL_EDGE = 8
VMEM_LIMIT = 60 * 1024 * 1024


def _dot(a, b):
    return jnp.dot(a, b, preferred_element_type=F32)


def _rms(x, g):
    ms = jnp.mean(x * x, axis=-1, keepdims=True)
    return x * lax.rsqrt(ms + EPS) * g


def _sigmoid(x):
    return 0.5 * jnp.tanh(0.5 * x) + 0.5


def _norm_modulate(x, ng_ref, mod_ref, stage):
    g = ng_ref[2 * stage:2 * stage + 1, :]
    shift = mod_ref[3 * stage:3 * stage + 1, :]
    scale = mod_ref[3 * stage + 1:3 * stage + 2, :]
    inv = lax.rsqrt(jnp.mean(x * x, axis=-1, keepdims=True) + EPS)
    return (x * inv) * (g * (1.0 + scale)) + shift


def _mod_row_map(tile_rows, ctx_rows, lat_rows):
    def row(tile):
        start = tile * tile_rows
        return jnp.where(start < ctx_rows, 0, 1 + (start - ctx_rows) // lat_rows)
    return row


def _mod_kernel(c_ref, w_ref, b_ref, o_ref):
    c = c_ref[...]
    s = c * _sigmoid(c)
    o_ref[...] = _dot(s.astype(BF16), w_ref[...].astype(BF16)) + b_ref[...]


def _modulation(cond, mod_w, mod_b):
    tn = MOD_TN
    n_col = (N_MOD * D_MODEL) // tn
    return pl.pallas_call(
        _mod_kernel,
        grid=(DEPTH, n_col),
        in_specs=[
            pl.BlockSpec((MOD_ROWS, D_MODEL), lambda l, n: (0, 0)),
            pl.BlockSpec((None, D_MODEL, tn), lambda l, n: (l, 0, n)),
            pl.BlockSpec((None, 1, tn), lambda l, n: (l, 0, n)),
        ],
        out_specs=pl.BlockSpec((None, MOD_ROWS, tn), lambda l, n: (l, 0, n)),
        out_shape=jax.ShapeDtypeStruct((DEPTH, MOD_ROWS, N_MOD * D_MODEL), F32),
        compiler_params=pltpu.CompilerParams(
            dimension_semantics=("arbitrary", "arbitrary"), vmem_limit_bytes=VMEM_LIMIT),
        name="modulation",
    )(cond, mod_w, mod_b.reshape(DEPTH, 1, N_MOD * D_MODEL))


def _ffn_kernel(x_ref, xn_ref, mod_ref, modn_ref, ng_ref, w1_ref, w3_ref, w2_ref, o_ref, hn_ref, p_ref,
                *, stage):
    @pl.when(pl.program_id(0) == 0)
    def _():
        hn_ref[...] = _norm_modulate(x_ref[...], ng_ref, mod_ref, stage).astype(BF16)

    gain = ng_ref[2 * stage + 1:2 * stage + 2, :] * (0.5 * mod_ref[3 * stage + 2:3 * stage + 3, :])
    n_blocks = FFN_TM // FFN_RB
    block_rows = [slice(r * FFN_RB, (r + 1) * FFN_RB) for r in range(n_blocks)]

    def up_project(rows):
        for k in range(FFN_NA):
            cols = slice(k * FFN_TF, (k + 1) * FFN_TF)
            a = _dot(hn_ref[rows, :], w1_ref[:, cols])
            b = _dot(hn_ref[rows, :], w3_ref[:, cols])
            p_ref[k, rows, :] = (a * _sigmoid(a) * b).astype(BF16)

    def down_project(r):
        rows = block_rows[r]
        piece = FFN_RB // FFN_NB
        accs = []
        for n in range(FFN_NB):
            nrows = slice(r * FFN_RB + n * piece, r * FFN_RB + (n + 1) * piece)
            hn_ref[nrows, :] = _norm_modulate(xn_ref[nrows, :], ng_ref, modn_ref, stage).astype(BF16)
            cols = slice(n * FFN_TN, (n + 1) * FFN_TN)
            acc = _dot(p_ref[0, rows, :], w2_ref[0:FFN_TF, cols])
            for k in range(1, FFN_NA):
                acc += _dot(p_ref[k, rows, :], w2_ref[k * FFN_TF:(k + 1) * FFN_TF, cols])
            accs.append(acc)
        ss = sum(jnp.sum(acc * acc, axis=-1, keepdims=True) for acc in accs)
        inv = lax.rsqrt(ss * (1.0 / D_MODEL) + EPS)
        for n, acc in enumerate(accs):
            cols = slice(n * FFN_TN, (n + 1) * FFN_TN)
            o_ref[rows, cols] = x_ref[rows, cols] + (acc * inv) * gain[:, cols]

    up_project(block_rows[0])
    for r in range(n_blocks):
        if r + 1 < n_blocks:
            up_project(block_rows[r + 1])
        down_project(r)


def _ffn(x, mod, norm_g, w1, w3, w2, *, layer, stage, row, tile0, n_tiles):
    sub = 0 if stage == 0 else 1
    cur = lambda i: tile0 + i
    nxt = lambda i: tile0 + jnp.minimum(i + 1, n_tiles - 1)
    resident = dict(pipeline_mode=pl.Buffered(1))
    return pl.pallas_call(
        functools.partial(_ffn_kernel, stage=stage),
        grid=(n_tiles,),
        in_specs=[
            pl.BlockSpec((FFN_TM, D_MODEL), lambda i: (cur(i), 0)),
            pl.BlockSpec((FFN_TM, D_MODEL), lambda i: (nxt(i), 0)),
            pl.BlockSpec((None, None, N_MOD, D_MODEL), lambda i: (layer, row(cur(i)), 0, 0)),
            pl.BlockSpec((None, None, N_MOD, D_MODEL), lambda i: (layer, row(nxt(i)), 0, 0)),
            pl.BlockSpec((None, 6, D_MODEL), lambda i: (layer, 0, 0)),
            pl.BlockSpec((None, None, D_MODEL, D_FF), lambda i: (layer, sub, 0, 0), **resident),
            pl.BlockSpec((None, None, D_MODEL, D_FF), lambda i: (layer, sub, 0, 0), **resident),
            pl.BlockSpec((None, None, D_FF, D_MODEL), lambda i: (layer, sub, 0, 0), **resident),
        ],
        out_specs=pl.BlockSpec((FFN_TM, D_MODEL), lambda i: (i, 0)),
        out_shape=jax.ShapeDtypeStruct((n_tiles * FFN_TM, D_MODEL), F32),
        scratch_shapes=[
            pltpu.VMEM((FFN_TM, D_MODEL), BF16),
            pltpu.VMEM((FFN_NA, FFN_TM, FFN_TF), BF16),
        ],
        compiler_params=pltpu.CompilerParams(
            dimension_semantics=("arbitrary",), vmem_limit_bytes=VMEM_LIMIT),
        name="ffn",
    )(x, x, mod, mod, norm_g, w1, w3, w2)


def _out_proj_kernel(x_ref, yc_ref, yl_ref, mod_ref, ng_ref, w_ref, o_ref, *, n_ctx):
    y_in = jnp.where(pl.program_id(0) < n_ctx, yc_ref[...], yl_ref[...])
    y = _dot(y_in, w_ref[...])
    o_ref[...] = x_ref[...] + mod_ref[5:6, :] * _rms(y, ng_ref[3:4, :])


def _out_proj(x, y_ctx, y_lat, mod, norm_g, w_out, *, layer, j, row):
    n_ctx = y_ctx.shape[0] // RES_TM
    n_lat = y_lat.shape[0] // RES_TM
    return pl.pallas_call(
        functools.partial(_out_proj_kernel, n_ctx=n_ctx),
        grid=(n_ctx + n_lat,),
        in_specs=[
            pl.BlockSpec((RES_TM, D_MODEL), lambda i: (i, 0)),
            pl.BlockSpec((RES_TM, D_RNN), lambda i: (jnp.minimum(i, n_ctx - 1), 0)),
            pl.BlockSpec((RES_TM, D_RNN), lambda i: (jnp.maximum(i - n_ctx, 0), 0)),
            pl.BlockSpec((None, None, N_MOD, D_MODEL), lambda i: (layer, row(i), 0, 0)),
            pl.BlockSpec((None, 6, D_MODEL), lambda i: (layer, 0, 0)),
            pl.BlockSpec((None, D_RNN, D_MODEL), lambda i: (j, 0, 0)),
        ],
        out_specs=pl.BlockSpec((RES_TM, D_MODEL), lambda i: (i, 0)),
        out_shape=jax.ShapeDtypeStruct(x.shape, F32),
        compiler_params=pltpu.CompilerParams(
            dimension_semantics=("parallel",), vmem_limit_bytes=VMEM_LIMIT),
        name="out_proj",
    )(x, y_ctx, y_lat, mod, norm_g, w_out)


def _seg_base(k):
    return SEG_GAP + k * SEG_PITCH


def _seg_rows(k, offset=0):
    return pl.ds(_seg_base(k) + offset, SEG)


def _rg_kernel(x_ref, mod_ref, ng_ref, h0f_ref, h0b_ref, wig_ref, wix_ref, cw_ref, cb_ref, wg_ref,
               ba_ref, bx_ref, lam_ref, *rest, chunked):
    if chunked:
        y_ref, hn_ref, gg_ref, xr_ref, a_ref, u_ref, h_ref, p_ref = rest
    else:
        y_ref, sf_ref, sb_ref, hn_ref, gg_ref, xr_ref, a_ref, u_ref, h_ref = rest
    hp = pl.program_id(1)

    @pl.when(hp == 0)
    def _():
        hn_ref[...] = _norm_modulate(x_ref[...], ng_ref, mod_ref, 1).astype(BF16)

    wig = wig_ref[...]
    wix = wix_ref[...]
    seg_per_chunk = RG_CHUNK // SEG
    zero2 = jnp.zeros((2, LANES), F32)

    def project(c):
        rows = slice(c * RG_CHUNK, (c + 1) * RG_CHUNK)
        hn = hn_ref[rows, :]
        gg_ref[rows, :] = jax.nn.gelu(_dot(hn, wig))
        xr = _dot(hn, wix)
        for s in range(N_SLAB):
            lanes = slice(s * LANES, (s + 1) * LANES)
            for kk in range(seg_per_chunk):
                k = c * seg_per_chunk + kk
                base = _seg_base(k)
                seg = xr[kk * SEG:(kk + 1) * SEG, lanes]
                xr_ref[s, _seg_rows(k), :] = seg
                inside = chunked and k > 0
                xr_ref[s, base - SEG_GAP:base - SEG_GAP + 2, :] = seg[0:2, :] if inside else zero2
                inside = chunked and k < N_SEG - 1
                xr_ref[s, base + SEG_PITCH - 2:base + SEG_PITCH, :] = seg[SEG - 2:SEG, :] if inside else zero2
                if k == 0:
                    xr_ref[s, base - 2:base, :] = zero2
                if k == N_SEG - 1:
                    xr_ref[s, base + SEG:base + SEG + 2, :] = zero2

    z = -lam_ref[...]
    softplus = jnp.maximum(z, 0.0) + jnp.log1p(jnp.exp(-jnp.abs(z)))
    half_rate = (0.5 * (-RG_C)) * softplus
    half_ba = 0.5 * ba_ref[...]
    half_bx = 0.5 * bx_ref[...]
    cw = cw_ref[...]
    cb = cb_ref[...]

    def gates(k):
        taps = []
        for s in range(N_SLAB):
            lanes = slice(s * LANES, (s + 1) * LANES)
            acc = cb[:, lanes] + cw[0:1, lanes] * xr_ref[s, _seg_rows(k, -1), :]
            for j in range(1, CONV_W):
                acc += cw[j:j + 1, lanes] * xr_ref[s, _seg_rows(k, j - 1), :]
            taps.append(acc)
        xk = jnp.concatenate(taps, axis=1)
        pre = _dot(xk.astype(BF16), wg_ref[...])
        xh = 0.5 * xk
        for d in range(2):
            tr = jnp.tanh(pre[:, (2 * d) * HP_W:(2 * d + 1) * HP_W] + half_ba[d:d + 1, :])
            ti = jnp.tanh(pre[:, (2 * d + 1) * HP_W:(2 * d + 2) * HP_W] + half_bx[d:d + 1, :])
            a = jnp.exp(tr * half_rate[d:d + 1, :] + half_rate[d:d + 1, :])
            v = 1.0 - a * a
            beta = jnp.where(v > 0.0, v * lax.rsqrt(v), 0.0)
            u = beta * (xh * ti + xh)
            for s in range(N_SLAB):
                lanes = slice(s * LANES, (s + 1) * LANES)
                a_ref[d, s, _seg_rows(k), :] = a[:, lanes]
                u_ref[d, s, _seg_rows(k), :] = u[:, lanes]

    n_chunk = MIX_TT // RG_CHUNK
    project(0)
    for c in range(n_chunk):
        if c + 1 < n_chunk:
            project(c + 1)
        first = c * seg_per_chunk - 1
        for k in range(max(first, 0), first + seg_per_chunk):
            gates(k)
    gates(N_SEG - 1)

    def step(t, carry):
        hs, ps = carry
        new_h, new_p = [], []
        for d in range(2):
            row = SEG_GAP + (t if d == 0 else SEG - 1 - t)
            idx = pl.ds(row, N_SEG, stride=SEG_PITCH)
            for s in range(N_SLAB):
                a = a_ref[d, s, idx, :]
                h = a * hs[d * N_SLAB + s] + u_ref[d, s, idx, :]
                h_ref[d, s, idx, :] = h
                new_h.append(h)
                if chunked:
                    p = a * ps[d * N_SLAB + s]
                    p_ref[d, s, idx, :] = p
                    new_p.append(p)
        return tuple(new_h), tuple(new_p)

    if chunked:
        h_init = tuple(jnp.zeros((N_SEG, LANES), F32) for _ in range(2 * N_SLAB))
        p_init = tuple(jnp.ones((N_SEG, LANES), F32) for _ in range(2 * N_SLAB))
    else:
        h0 = (h0f_ref[...], h0b_ref[...])
        h_init = tuple(h0[d][:, s * LANES:(s + 1) * LANES] for d in range(2) for s in range(N_SLAB))
        p_init = ()
    h_fin, _ = lax.fori_loop(0, SEG, step, (h_init, p_init), unroll=8)

    if chunked:
        b = pl.program_id(0)
        h0 = (h0f_ref[pl.ds(b, 1), :], h0b_ref[pl.ds(b, 1), :])
        for s in range(N_SLAB):
            lanes = slice(s * LANES, (s + 1) * LANES)
            carry = h0[0][:, lanes]
            for k in range(N_SEG):
                h = h_ref[0, s, _seg_rows(k), :] + p_ref[0, s, _seg_rows(k), :] * carry
                h_ref[0, s, _seg_rows(k), :] = h
                carry = h[SEG - 1:SEG, :]
            carry = h0[1][:, lanes]
            for k in reversed(range(N_SEG)):
                hb = h_ref[1, s, _seg_rows(k), :] + p_ref[1, s, _seg_rows(k), :] * carry
                carry = hb[0:1, :]
                rows = slice(k * SEG, (k + 1) * SEG)
                y = (h_ref[0, s, _seg_rows(k), :] + hb) * gg_ref[rows, lanes]
                y_ref[rows, lanes] = y.astype(BF16)
    else:
        for s in range(N_SLAB):
            lanes = slice(s * LANES, (s + 1) * LANES)
            sf_ref[:, lanes] = h_fin[s]
            sb_ref[:, lanes] = h_fin[N_SLAB + s]
            for k in range(N_SEG):
                rows = slice(k * SEG, (k + 1) * SEG)
                y = (h_ref[0, s, _seg_rows(k), :] + h_ref[1, s, _seg_rows(k), :]) * gg_ref[rows, lanes]
                y_ref[rows, lanes] = y.astype(BF16)


def _rg_scan(x, mod, norm_g, h0f, h0b, rg, *, layer, j, chunked, row, tile0, n_tiles):
    t = n_tiles * MIX_TT
    w_in, conv_w, conv_b, w_gate, b_a, b_x, lam = rg
    n_state = h0f.shape[0]
    state_block = (n_state, HP_W) if chunked else (N_SEG, HP_W)
    state_map = (lambda i, hp: (0, hp)) if chunked else (lambda i, hp: (i, hp))
    in_specs = [
        pl.BlockSpec((MIX_TT, D_MODEL), lambda i, hp: (tile0 + i, 0)),
        pl.BlockSpec((None, None, N_MOD, D_MODEL), lambda i, hp: (layer, row(tile0 + i), 0, 0)),
        pl.BlockSpec((None, 6, D_MODEL), lambda i, hp: (layer, 0, 0)),
        pl.BlockSpec(state_block, state_map),
        pl.BlockSpec(state_block, state_map),
        pl.BlockSpec((None, D_MODEL, HP_W), lambda i, hp: (j, 0, hp)),
        pl.BlockSpec((None, D_MODEL, HP_W), lambda i, hp: (j, 0, N_HP + hp)),
        pl.BlockSpec((None, CONV_W, HP_W), lambda i, hp: (j, 0, hp)),
        pl.BlockSpec((None, 1, HP_W), lambda i, hp: (j, 0, hp)),
        pl.BlockSpec((None, None, HP_W, 4 * HP_W), lambda i, hp: (j, hp, 0, 0)),
        pl.BlockSpec((None, 2, HP_W), lambda i, hp: (j, 0, hp)),
        pl.BlockSpec((None, 2, HP_W), lambda i, hp: (j, 0, hp)),
        pl.BlockSpec((None, 2, HP_W), lambda i, hp: (j, 0, hp)),
    ]
    y_spec = pl.BlockSpec((MIX_TT, HP_W), lambda i, hp: (i, hp))
    y_shape = jax.ShapeDtypeStruct((t, D_RNN), BF16)
    if chunked:
        out_specs, out_shape = y_spec, y_shape
    else:
        s_spec = pl.BlockSpec((N_SEG, HP_W), lambda i, hp: (i, hp))
        s_shape = jax.ShapeDtypeStruct((t // SEG, D_RNN), F32)
        out_specs, out_shape = (y_spec, s_spec, s_spec), (y_shape, s_shape, s_shape)
    scan = pltpu.VMEM((2, N_SLAB, SCAN_ROWS, LANES), F32)
    scratch = [
        pltpu.VMEM((MIX_TT, D_MODEL), BF16),
        pltpu.VMEM((MIX_TT, HP_W), F32),
        pltpu.VMEM((N_SLAB, SCAN_ROWS, LANES), F32),
        scan, scan, scan,
    ] + ([scan] if chunked else [])
    return pl.pallas_call(
        functools.partial(_rg_kernel, chunked=chunked),
        grid=(t // MIX_TT, N_HP),
        in_specs=in_specs,
        out_specs=out_specs,
        out_shape=out_shape,
        scratch_shapes=scratch,
        compiler_params=pltpu.CompilerParams(
            dimension_semantics=("parallel", "arbitrary"), vmem_limit_bytes=VMEM_LIMIT),
        name="rg_scan",
    )(x, mod, norm_g, h0f, h0b, w_in, w_in, conv_w, conv_b, w_gate, b_a, b_x, lam)


def _block_diag_pairs(w):
    n = w.shape[0]
    w = w.reshape(n, 2, N_HP, 2, RNN_HEAD_DIM, RNN_HEAD_DIM)
    z = jnp.zeros_like(w[:, :, :, 0])
    top = jnp.concatenate([w[:, :, :, 0], z], axis=-1)
    bot = jnp.concatenate([z, w[:, :, :, 1]], axis=-1)
    return jnp.concatenate([top, bot], axis=-2)


def _pool_layout(on_grid):
    run, n_run, pad_runs = (GRID_W, MIX_TT // GRID_W, POOL_HALO) if on_grid else (SEG, N_SEG, 0)
    pitch = run + POOL_HALO
    rows = 2 * POOL_EDGE + POOL_HALO + (n_run + 2 * pad_runs) * pitch
    return run, n_run, pad_runs, pitch, rows


def _window_count(half, pos, length):
    return (jnp.minimum(pos + half, length) - jnp.maximum(pos - half, 0)).astype(F32)


def _pool_kernel(x_ref, mod_ref, ng_ref, pw0_ref, pw1_ref, ps0_ref, ps1_ref, o_ref,
                 inv_ref, d_ref, xb_ref, b0_ref, b1_ref, *, n_ctx):
    i = pl.program_id(0)
    g = pl.program_id(1)

    @pl.when(g == 0)
    def _():
        x = x_ref[...]
        inv_ref[...] = lax.rsqrt(jnp.mean(x * x, axis=-1, keepdims=True) + EPS)

    row_id = lax.broadcasted_iota(jnp.int32, (MIX_TT, 1), 0)
    zero_edge = jnp.zeros((POOL_EDGE, LANES), F32)

    def pool_group(gi, w, on_grid):
        run, n_run, pad_runs, pitch, n_rows = _pool_layout(on_grid)
        base = lambda r: POOL_EDGE + POOL_HALO + (r + pad_runs) * pitch
        levels = w.bit_length() - 1
        half = w // 2

        def window_sums(s):
            src, bufs = xb_ref, [b0_ref, b1_ref]
            lo, hi = POOL_EDGE, n_rows - POOL_EDGE
            for lvl in range(levels):
                dst = bufs[lvl % 2]
                dn, up = (1, 0) if lvl == 0 else (2 ** (lvl - 1),) * 2
                dst[s, lo:hi, :] = src[s, lo - dn:hi - dn, :] + src[s, lo + up:hi + up, :]
                src = dst
            if on_grid:
                reach = 0
                for lvl in range(levels):
                    dst = bufs[(levels + lvl) % 2]
                    dn, up = (1, 0) if lvl == 0 else (2 ** (lvl - 1),) * 2
                    reach += dn
                    lo, hi = base(reach - pad_runs), base(n_run + pad_runs - reach)
                    dst[s, lo:hi, :] = (src[s, lo - dn * pitch:hi - dn * pitch, :]
                                        + src[s, lo + up * pitch:hi + up * pitch, :])
                    src = dst
            return src

        cols = slice(gi * POOL_GROUP_DIM, (gi + 1) * POOL_GROUP_DIM)
        scale = ng_ref[2:3, cols] * (1.0 + mod_ref[4:5, cols])
        xg = (x_ref[:, cols] * inv_ref[...]) * scale + mod_ref[3:4, cols]
        if on_grid:
            cnt = (_window_count(half, row_id % GRID_W, GRID_W)
                   * _window_count(half, row_id // GRID_W, n_run))
        else:
            cnt = _window_count(half, row_id % SEG, SEG)
        inv_cnt = 1.0 / cnt
        for s in range(POOL_GROUP_DIM // LANES):
            lanes = slice(s * LANES, (s + 1) * LANES)
            xb_ref[s, 0:n_rows, :] = jnp.zeros((n_rows, LANES), F32)
            for r in range(n_run):
                xb_ref[s, base(r):base(r) + run, :] = xg[r * run:(r + 1) * run, lanes]
            for buf in (b0_ref, b1_ref):
                buf[s, 0:POOL_EDGE, :] = zero_edge
                buf[s, n_rows - POOL_EDGE:n_rows, :] = zero_edge
            tot = window_sums(s)
            for r in range(n_run):
                rows = slice(r * run, (r + 1) * run)
                mean = tot[s, base(r):base(r) + run, :] * inv_cnt[rows, :]
                d_ref[rows, lanes] = (mean - xb_ref[s, base(r):base(r) + run, :]).astype(BF16)
        pw_ref, ps_ref = ((pw0_ref, ps0_ref), (pw1_ref, ps1_ref))[gi % 2]
        o_ref[:, cols] = _dot(d_ref[...], pw_ref[...]) * ps_ref[...]

    for on_grid in (False, True):
        for gi, w in enumerate(POOL_WINDOWS):
            pl.when((g == gi // 2) & ((i >= n_ctx) == on_grid))(functools.partial(pool_group, gi, w, on_grid))

    @pl.when(g == POOL_GROUPS // 2 - 1)
    def _():
        gain = mod_ref[5:6, :] * ng_ref[3:4, :]
        for r in range(MIX_TT // SEG):
            rows = slice(r * SEG, (r + 1) * SEG)
            y = o_ref[rows, :]
            inv = lax.rsqrt(jnp.mean(y * y, axis=-1, keepdims=True) + EPS)
            o_ref[rows, :] = x_ref[rows, :] + (y * inv) * gain


def _pool_mix(x, mod, norm_g, pool_w, pool_scale, *, layer, j, row, n_ctx):
    t = x.shape[0]
    n_rows = max(_pool_layout(False)[-1], _pool_layout(True)[-1])
    buf = pltpu.VMEM((POOL_GROUP_DIM // LANES, n_rows, LANES), F32)
    return pl.pallas_call(
        functools.partial(_pool_kernel, n_ctx=n_ctx),
        grid=(t // MIX_TT, POOL_GROUPS // 2),
        in_specs=[
            pl.BlockSpec((MIX_TT, D_MODEL), lambda i, g: (i, 0)),
            pl.BlockSpec((None, None, N_MOD, D_MODEL), lambda i, g: (layer, row(i), 0, 0)),
            pl.BlockSpec((None, 6, D_MODEL), lambda i, g: (layer, 0, 0)),
            pl.BlockSpec((None, None, POOL_GROUP_DIM, POOL_GROUP_DIM), lambda i, g: (j, 2 * g, 0, 0)),
            pl.BlockSpec((None, None, POOL_GROUP_DIM, POOL_GROUP_DIM), lambda i, g: (j, 2 * g + 1, 0, 0)),
            pl.BlockSpec((None, 1, POOL_GROUP_DIM), lambda i, g: (j, 0, 2 * g)),
            pl.BlockSpec((None, 1, POOL_GROUP_DIM), lambda i, g: (j, 0, 2 * g + 1)),
        ],
        out_specs=pl.BlockSpec((MIX_TT, D_MODEL), lambda i, g: (i, 0)),
        out_shape=jax.ShapeDtypeStruct((t, D_MODEL), F32),
        scratch_shapes=[
            pltpu.VMEM((MIX_TT, 1), F32),
            pltpu.VMEM((MIX_TT, POOL_GROUP_DIM), BF16),
            buf, buf, buf,
        ],
        compiler_params=pltpu.CompilerParams(
            dimension_semantics=("parallel", "arbitrary"), vmem_limit_bytes=VMEM_LIMIT),
        name="pool_mix",
    )(x, mod, norm_g, pool_w, pool_w, pool_scale, pool_scale)


def kernel(x_prompt, x_sample, state_rglru, c, c_ctx, mod_w, mod_b, norm_g, ffn_w1, ffn_w3, ffn_w2,
           rg_w_in, rg_conv_w, rg_conv_b, rg_w_a, rg_b_a, rg_w_x, rg_b_x, rg_lam, rg_w_out,
           pool_w, pool_scale):
    batch, seq, _ = x_prompt.shape
    dec_batch, dec_seq, _ = x_sample.shape
    ctx_rows, lat_rows = batch * seq, dec_batch * dec_seq
    assert seq == SEG and dec_seq == MIX_TT and dec_seq % GRID_W == 0
    assert ctx_rows % MIX_TT == 0 and MIX_TT % FFN_TM == 0 and FFN_TM == RES_TM
    assert 1 + dec_batch <= MOD_ROWS
    n_rg = rg_w_in.shape[0]

    cond = jnp.zeros((MOD_ROWS, D_MODEL), F32).at[0].set(c_ctx).at[1:1 + dec_batch].set(c)
    mod = _modulation(cond, mod_w, mod_b).reshape(DEPTH, MOD_ROWS, N_MOD, D_MODEL)

    wa, wx = _block_diag_pairs(rg_w_a), _block_diag_pairs(rg_w_x)
    w_gate = (0.5 * jnp.concatenate([wa[:, 0], wx[:, 0], wa[:, 1], wx[:, 1]], axis=-1)).astype(BF16)
    rg = (rg_w_in.astype(BF16), rg_conv_w, rg_conv_b.reshape(n_rg, 1, D_RNN), w_gate, rg_b_a, rg_b_x, rg_lam)
    pool_scale3 = pool_scale.reshape(pool_scale.shape[0], 1, D_MODEL)
    zeros = jnp.zeros((batch, D_RNN), F32)
    ffn_w1, ffn_w3, ffn_w2, rg_w_out, pool_w = (
        w.astype(BF16) for w in (ffn_w1, ffn_w3, ffn_w2, rg_w_out, pool_w))

    x = jnp.concatenate([x_prompt.reshape(ctx_rows, D_MODEL), x_sample.reshape(lat_rows, D_MODEL)], axis=0)
    ffn_row = _mod_row_map(FFN_TM, ctx_rows, dec_seq)
    mix_row = _mod_row_map(MIX_TT, ctx_rows, dec_seq)
    n_ffn_ctx, n_ffn_lat = ctx_rows // FFN_TM, lat_rows // FFN_TM
    n_mix_ctx, n_mix_lat = ctx_rows // MIX_TT, lat_rows // MIX_TT
    ffn = functools.partial(_ffn, mod=mod, norm_g=norm_g, w1=ffn_w1, w3=ffn_w3, w2=ffn_w2, row=ffn_row)

    ctx_states = []
    for l in range(DEPTH):
        j = l // 2
        x = ffn(x, layer=l, stage=0, tile0=0, n_tiles=n_ffn_ctx + n_ffn_lat)
        if l % 2 == 0:
            scan = functools.partial(_rg_scan, x, mod, norm_g, rg=rg, layer=l, j=j, row=mix_row)
            y_ctx, sf, sb = scan(zeros, zeros, chunked=False, tile0=0, n_tiles=n_mix_ctx)
            y_lat = scan(state_rglru[:, j, 0], state_rglru[:, j, 1], chunked=True,
                         tile0=n_mix_ctx, n_tiles=n_mix_lat)
            ctx_states.append(jnp.stack([sf, sb], axis=1))
            x = _out_proj(x, y_ctx, y_lat, mod, norm_g, rg_w_out, layer=l, j=j, row=ffn_row)
        else:
            x = _pool_mix(x, mod, norm_g, pool_w, pool_scale3, layer=l, j=j, row=mix_row, n_ctx=n_mix_ctx)
        if l + 1 < DEPTH:
            x = ffn(x, layer=l, stage=2, tile0=0, n_tiles=n_ffn_ctx + n_ffn_lat)
    y_prompt = ffn(x, layer=DEPTH - 1, stage=2, tile0=0, n_tiles=n_ffn_ctx)
    y_sample = ffn(x, layer=DEPTH - 1, stage=2, tile0=n_ffn_ctx, n_tiles=n_ffn_lat)
    new_state = jnp.stack(ctx_states, axis=1)
    return (y_prompt.reshape(batch, seq, D_MODEL), y_sample.reshape(dec_batch, dec_seq, D_MODEL), new_state)
```
